```python
import jax, jax.numpy as jnp
from jax import lax
import numpy as np

D_MODEL = 1024
BATCH = 16
SEQ = 256
DEPTH = 2
DEC_BATCH = 8
DEC_SEQ = 4096
PAST_LEN = 512

GRID_W = 64
HEAD_DIM = 64
ATTN_WIDTH = D_MODEL // 2
N_HEADS = ATTN_WIDTH // HEAD_DIM
N_KV_HEADS = 2
GQA_GROUP = N_HEADS // N_KV_HEADS
KV_WIDTH = N_KV_HEADS * HEAD_DIM
POOL_WIDTH = D_MODEL // 4
POOL_WINDOWS = (2, 4, 8, 16)
POOL_GROUP = POOL_WIDTH // len(POOL_WINDOWS)
CONV_WIDTH = D_MODEL // 4
CONV_K = 3
WINDOW = 128
BLOCK = 128
D_FF = 2816
ROPE_BASE = 10000.0
EPS = 1e-6
NEG = -1e30
N_MOD = 9
MIX_WIDTH = ATTN_WIDTH + POOL_WIDTH + CONV_WIDTH
IN_WIDTH = ATTN_WIDTH + 2 * KV_WIDTH + POOL_WIDTH + 3 * CONV_WIDTH
IN_SPLITS = (ATTN_WIDTH, ATTN_WIDTH + KV_WIDTH, ATTN_WIDTH + 2 * KV_WIDTH,
             ATTN_WIDTH + 2 * KV_WIDTH + POOL_WIDTH,
             ATTN_WIDTH + 2 * KV_WIDTH + POOL_WIDTH + CONV_WIDTH,
             ATTN_WIDTH + 2 * KV_WIDTH + POOL_WIDTH + 2 * CONV_WIDTH)

kernel_name = "hybrid_pool_conv_swa_diffusion_step"


def rms_norm(x, g):
    xf = x.astype(jnp.float32)
    y = xf * lax.rsqrt(jnp.mean(xf * xf, axis=-1, keepdims=True) + EPS)
    return (y * g.astype(jnp.float32)).astype(x.dtype)


def ada_params(cond, w_ada, b_ada):
    m = jax.nn.silu(cond) @ w_ada + b_ada
    return m.reshape(m.shape[:-1] + (N_MOD, D_MODEL))


def modulated_norm(x, mod, k, g):
    shift, scale = mod[:, None, 3 * k], mod[:, None, 3 * k + 1]
    return rms_norm(x, g) * (1 + scale) + shift


def ffn_sublayer(x, mod, k, g, w_in, w_out):
    h = modulated_norm(x, mod, k, g)
    gt, up = jnp.split(h @ w_in, 2, axis=-1)
    return x + 0.5 * mod[:, None, 3 * k + 2] * ((jax.nn.silu(gt) * up) @ w_out)


def multiscale_pool(u, pool_w, pool_scale):
    T = u.shape[1]
    uf = u.astype(jnp.float32)
    cs = jnp.concatenate([jnp.zeros_like(uf[:, :1]), jnp.cumsum(uf, axis=1)], axis=1)
    t = jnp.arange(T)
    outs = []
    for gi, w in enumerate(POOL_WINDOWS):
        lo = jnp.clip(t - w // 2, 0, T)
        hi = jnp.clip(t + w // 2, 0, T)
        sl = slice(gi * POOL_GROUP, (gi + 1) * POOL_GROUP)
        csg = cs[..., sl]
        cnt = (hi - lo).astype(jnp.float32)[None, :, None]
        pooled = (jnp.take(csg, hi, axis=1) - jnp.take(csg, lo, axis=1)) / cnt - uf[..., sl]
        outs.append(jnp.einsum('btc,cd->btd', pooled.astype(u.dtype), pool_w[gi]))
    return jnp.concatenate(outs, axis=-1) * pool_scale


def short_conv(u, bgate, cgate, conv_w):
    T = u.shape[1]
    zp = jnp.pad(cgate * u, ((0, 0), (1, 1), (0, 0)))
    conv = zp[:, 0:T] * conv_w[0] + zp[:, 1:T + 1] * conv_w[1] + zp[:, 2:T + 2] * conv_w[2]
    return bgate * conv


def axial_rope(x):
    T = x.shape[1]
    rows = T // GRID_W
    row = jnp.repeat(jnp.arange(rows), GRID_W).astype(jnp.float32)
    col = jnp.tile(jnp.arange(GRID_W), rows).astype(jnp.float32)
    half = HEAD_DIM // 2
    inv = ROPE_BASE ** (-jnp.arange(0, half, 2, dtype=jnp.float32) / half)

    def rot(xa, pos):
        ang = pos[:, None] * inv[None, :]
        cos = jnp.cos(ang)[None, :, None, :]
        sin = jnp.sin(ang)[None, :, None, :]
        x1, x2 = jnp.split(xa, 2, axis=-1)
        return jnp.concatenate([x1 * cos - x2 * sin, x1 * sin + x2 * cos], axis=-1)

    xf = x.astype(jnp.float32)
    out = jnp.concatenate([rot(xf[..., :half], row), rot(xf[..., half:], col)], axis=-1)
    return out.astype(x.dtype)


def context_attention(q, k, v, sink):
    Bn, T = q.shape[:2]
    qg = q.reshape(Bn, T, N_KV_HEADS, GQA_GROUP, HEAD_DIM)
    s = jnp.einsum('bqkgd,bskd->bkgqs', qg, k).astype(jnp.float32) * (HEAD_DIM ** -0.5)
    s_sink = jnp.broadcast_to(sink.astype(jnp.float32).reshape(1, N_KV_HEADS, GQA_GROUP, 1, 1), s.shape[:-1] + (1,))
    p = jax.nn.softmax(jnp.concatenate([s, s_sink], axis=-1), axis=-1).astype(v.dtype)
    o = jnp.einsum('bkgqs,bskd->bqkgd', p[..., :T], v)
    return o.reshape(Bn, T, ATTN_WIDTH)


def windowed_attention(q, k, v, ck, cv, sink):
    Bn, T = q.shape[:2]
    C = ck.shape[1]
    nb = T // BLOCK
    L = 3 * BLOCK
    qb = q.reshape(Bn, nb, BLOCK, N_KV_HEADS, GQA_GROUP, HEAD_DIM)

    def band(a):
        ap = jnp.pad(a, ((0, 0), (BLOCK, BLOCK), (0, 0), (0, 0))).reshape(Bn, nb + 2, BLOCK, N_KV_HEADS, HEAD_DIM)
        return jnp.concatenate([ap[:, :nb], ap[:, 1:nb + 1], ap[:, 2:]], axis=2)

    kb, vb = band(k), band(v)
    qi = jnp.arange(BLOCK)[:, None]
    sj = jnp.arange(L)[None, :]
    kpos = jnp.arange(nb)[:, None, None] * BLOCK - BLOCK + sj[None]
    valid = (jnp.abs(sj - BLOCK - qi) <= WINDOW)[None] & (kpos >= 0) & (kpos < T)
    scale = HEAD_DIM ** -0.5
    s_loc = jnp.einsum('bnqkgd,bnskd->bnkgqs', qb, kb).astype(jnp.float32) * scale
    s_loc = jnp.where(valid[None, :, None, None], s_loc, NEG)
    s_ctx = jnp.einsum('bnqkgd,bckd->bnkgqc', qb, ck).astype(jnp.float32) * scale
    s_sink = jnp.broadcast_to(sink.astype(jnp.float32).reshape(1, 1, N_KV_HEADS, GQA_GROUP, 1, 1), s_loc.shape[:-1] + (1,))
    p = jax.nn.softmax(jnp.concatenate([s_loc, s_ctx, s_sink], axis=-1), axis=-1).astype(v.dtype)
    o = (jnp.einsum('bnkgqs,bnskd->bnqkgd', p[..., :L], vb)
         + jnp.einsum('bnkgqc,bckd->bnqkgd', p[..., L:L + C], cv))
    return o.reshape(Bn, T, ATTN_WIDTH)


def trunk_layer(x, mod, norm_g, w_ffn_in, w_ffn_out, w_in, w_out, q_norm_g, k_norm_g, sink,
                pool_w, pool_scale, conv_w, ctx_k, ctx_v, latent):
    Bn, T, _ = x.shape
    x = ffn_sublayer(x, mod, 0, norm_g[0], w_ffn_in[0], w_ffn_out[0])
    h = modulated_norm(x, mod, 1, norm_g[1])
    q, k, v, u_pool, u_conv, bg, cg = jnp.split(h @ w_in, IN_SPLITS, axis=-1)
    q = rms_norm(q.reshape(Bn, T, N_HEADS, HEAD_DIM), q_norm_g)
    k = rms_norm(k.reshape(Bn, T, N_KV_HEADS, HEAD_DIM), k_norm_g)
    v = v.reshape(Bn, T, N_KV_HEADS, HEAD_DIM)
    if latent:
        attn = windowed_attention(axial_rope(q), axial_rope(k), v, ctx_k, ctx_v, sink)
    else:
        attn = context_attention(q, k, v, sink)
    pool = multiscale_pool(u_pool, pool_w, pool_scale)
    conv = short_conv(u_conv, bg, cg, conv_w)
    y = jnp.concatenate([attn, pool, conv], axis=-1) @ w_out
    x = x + mod[:, None, 5] * y
    x = ffn_sublayer(x, mod, 2, norm_g[2], w_ffn_in[1], w_ffn_out[1])
    return x, k, v


def setup_inputs(seed: int = 0) -> dict:
    key = jax.random.key(seed)
    ks = jax.random.split(key, 20)
    f32 = jnp.float32
    nrm = lambda k, s: jax.random.normal(k, s, f32)
    return {
        'x_prompt': nrm(ks[0], (BATCH, SEQ, D_MODEL)),
        'x_sample': nrm(ks[1], (DEC_BATCH, DEC_SEQ, D_MODEL)),
        'cache_k': nrm(ks[2], (DEC_BATCH, DEPTH, PAST_LEN, N_KV_HEADS, HEAD_DIM)),
        'cache_v': nrm(ks[3], (DEC_BATCH, DEPTH, PAST_LEN, N_KV_HEADS, HEAD_DIM)),
        'c': nrm(ks[4], (DEC_BATCH, D_MODEL)),
        'c_ctx': nrm(ks[5], (D_MODEL,)),
        'w_ada': nrm(ks[6], (DEPTH, D_MODEL, N_MOD * D_MODEL)) * (0.2 * D_MODEL ** -0.5),
        'b_ada': nrm(ks[7], (DEPTH, N_MOD * D_MODEL)) * 0.02,
        'norm_g': 1.0 + 0.02 * nrm(ks[8], (DEPTH, 3, D_MODEL)),
        'w_ffn_in': nrm(ks[9], (DEPTH, 2, D_MODEL, 2 * D_FF)) * D_MODEL ** -0.5,
        'w_ffn_out': nrm(ks[10], (DEPTH, 2, D_FF, D_MODEL)) * D_FF ** -0.5,
        'w_in': nrm(ks[11], (DEPTH, D_MODEL, IN_WIDTH)) * D_MODEL ** -0.5,
        'w_out': nrm(ks[12], (DEPTH, MIX_WIDTH, D_MODEL)) * MIX_WIDTH ** -0.5,
        'q_norm_g': 1.0 + 0.02 * nrm(ks[13], (DEPTH, HEAD_DIM)),
        'k_norm_g': 1.0 + 0.02 * nrm(ks[14], (DEPTH, HEAD_DIM)),
        'sink': 0.5 * nrm(ks[15], (DEPTH, N_HEADS)),
        'pool_w': nrm(ks[16], (DEPTH, len(POOL_WINDOWS), POOL_GROUP, POOL_GROUP)) * POOL_GROUP ** -0.5,
        'pool_scale': 1.0 + 0.02 * nrm(ks[17], (DEPTH, POOL_WIDTH)),
        'conv_w': nrm(ks[18], (DEPTH, CONV_K, CONV_WIDTH)) * CONV_K ** -0.5,
    }


def reference(x_prompt, x_sample, cache_k, cache_v, c, c_ctx, w_ada, b_ada, norm_g, w_ffn_in, w_ffn_out,
              w_in, w_out, q_norm_g, k_norm_g, sink, pool_w, pool_scale, conv_w):
    xp, xs = x_prompt, x_sample
    new_k, new_v = [], []
    for l in range(DEPTH):
        mod_ctx = ada_params(c_ctx[None, :], w_ada[l], b_ada[l])
        mod_lat = ada_params(c, w_ada[l], b_ada[l])
        xp, kl, vl = trunk_layer(xp, mod_ctx, norm_g[l], w_ffn_in[l], w_ffn_out[l], w_in[l], w_out[l],
                                 q_norm_g[l], k_norm_g[l], sink[l], pool_w[l], pool_scale[l], conv_w[l],
                                 None, None, False)
        new_k.append(kl)
        new_v.append(vl)
        xs, _, _ = trunk_layer(xs, mod_lat, norm_g[l], w_ffn_in[l], w_ffn_out[l], w_in[l], w_out[l],
                               q_norm_g[l], k_norm_g[l], sink[l], pool_w[l], pool_scale[l], conv_w[l],
                               cache_k[:, l], cache_v[:, l], True)
    state_k = jnp.stack(new_k, axis=1)
    state_v = jnp.stack(new_v, axis=1)
    return (xp, xs, state_k, state_v)
```

```python
import functools

import jax
import jax.numpy as jnp
from jax import lax
from jax.experimental import pallas as pl
from jax.experimental.pallas import tpu as pltpu

D_MODEL = 1024
DEPTH = 2
GRID_W = 64
HEAD_DIM = 64
ATTN_WIDTH = 512
N_HEADS = 8
N_KV_HEADS = 2
GQA_GROUP = 4
KV_WIDTH = 128
POOL_WIDTH = 256
POOL_WINDOWS = (2, 4, 8, 16)
POOL_GROUP = 64
CONV_WIDTH = 256
WINDOW = 128
BLOCK = 128
D_FF = 2816
ROPE_BASE = 10000.0
EPS = 1e-6
NEG = -1e30
N_MOD = 9
IN_WIDTH = 1792
MIX_IN_WIDTH = POOL_WIDTH + 2 * CONV_WIDTH

LANES = 128
HALO = 8
MXU_COLS = 256
FF_CHUNK = MXU_COLS
N_FF_CHUNKS = D_FF // FF_CHUNK
COND_ROWS = 16
ADA_CHUNK = 1152
VMEM_LIMIT = 56 * 1024 * 1024

_BF16 = jnp.bfloat16
_F32 = jnp.float32


def _dot(a, b):
    return jnp.dot(a, b, preferred_element_type=_F32)


def _dot_nt(a, b):
    return lax.dot_general(a, b, (((1,), (1,)), ((), ())), preferred_element_type=_F32)


def _silu(x):
    return x * (1.0 / (1.0 + jnp.exp(-x)))


def _modulated_norm(x, g, shift, scale):
    y = x * lax.rsqrt(jnp.mean(x * x, axis=-1, keepdims=True) + EPS)
    return (y * g) * (1.0 + scale) + shift


def _ada_kernel(cond_ref, w_ref, b_ref, o_ref):
    s = _silu(cond_ref[...]).astype(_BF16)
    o_ref[...] = _dot(s, w_ref[...].astype(_BF16)) + b_ref[...]


def _ada_rows(cond, w_ada, b_ada):
    n_out = N_MOD * D_MODEL
    return pl.pallas_call(
        _ada_kernel,
        grid=(DEPTH, n_out // ADA_CHUNK),
        in_specs=[
            pl.BlockSpec((COND_ROWS, D_MODEL), lambda l, j: (0, 0)),
            pl.BlockSpec((None, D_MODEL, ADA_CHUNK), lambda l, j: (l, 0, j)),
            pl.BlockSpec((None, 1, ADA_CHUNK), lambda l, j: (l, 0, j)),
        ],
        out_specs=pl.BlockSpec((None, COND_ROWS, ADA_CHUNK), lambda l, j: (l, 0, j)),
        out_shape=jax.ShapeDtypeStruct((DEPTH, COND_ROWS, n_out), _F32),
        name="ada_rows",
        compiler_params=pltpu.CompilerParams(
            dimension_semantics=("arbitrary", "arbitrary"), vmem_limit_bytes=VMEM_LIMIT),
    )(cond, w_ada, b_ada.reshape(DEPTH, 1, n_out))


def _ffn_kernel(x_ref, mod_ref, g_ref, w_in_ref, w_out_ref, o_ref, *, k):
    x = x_ref[...]
    shift = mod_ref[3 * k:3 * k + 1, :]
    scale = mod_ref[3 * k + 1:3 * k + 2, :]
    gate = mod_ref[3 * k + 2:3 * k + 3, :]
    h = _modulated_norm(x, g_ref[...], shift, scale).astype(_BF16)
    acc = jnp.zeros(x.shape, _F32)
    for c in range(N_FF_CHUNKS):
        gu = _dot(h, w_in_ref[:, 2 * FF_CHUNK * c:2 * FF_CHUNK * (c + 1)])
        a = (_silu(gu[:, :FF_CHUNK]) * gu[:, FF_CHUNK:]).astype(_BF16)
        acc = acc + _dot(a, w_out_ref[FF_CHUNK * c:FF_CHUNK * (c + 1), :])
    o_ref[...] = x + (0.5 * gate) * acc


def _ffn(x2d, mods, layer, mod_row_of_tile, k, g, w_in, w_out, tm):
    n_tok = x2d.shape[0]
    resident = functools.partial(pl.BlockSpec, pipeline_mode=pl.Buffered(1))
    return pl.pallas_call(
        functools.partial(_ffn_kernel, k=k),
        grid=(n_tok // tm,),
        in_specs=[
            pl.BlockSpec((tm, D_MODEL), lambda i: (i, 0)),
            pl.BlockSpec((None, None, N_MOD, D_MODEL), lambda i: (layer, mod_row_of_tile(i), 0, 0)),
            pl.BlockSpec((1, D_MODEL), lambda i: (0, 0)),
            resident((D_MODEL, 2 * D_FF), lambda i: (0, 0)),
            resident((D_FF, D_MODEL), lambda i: (0, 0)),
        ],
        out_specs=pl.BlockSpec((tm, D_MODEL), lambda i: (i, 0)),
        out_shape=jax.ShapeDtypeStruct(x2d.shape, _F32),
        name="ffn_half_step",
        compiler_params=pltpu.CompilerParams(
            dimension_semantics=("arbitrary",), vmem_limit_bytes=VMEM_LIMIT),
    )(x2d, mods, g, w_in, w_out)


def _head_mean_sq(v, seg_ref):
    return _dot((v * v).astype(_BF16), seg_ref[...])


def _rotate(v, cos, sin_lo, sin_hi):
    n = v.shape[-1]
    quarter = HEAD_DIM // 4
    return v * cos + pltpu.roll(v, n - quarter, 1) * sin_lo + pltpu.roll(v, quarter, 1) * sin_hi


def _proj_kernel(*refs, latent):
    if latent:
        (x_ref, mod_ref, g_ref, w_ref, gq_ref, gk_ref, seg_ref, cos_ref, slo_ref, shi_ref,
         q_ref, kd_ref, vd_ref, mix_ref) = refs
    else:
        (x_ref, mod_ref, g_ref, w_ref, gq_ref, gk_ref, seg_ref,
         q_ref, kd_ref, vd_ref, mix_ref, k32_ref, v32_ref) = refs
    x = x_ref[...]
    h = _modulated_norm(x, g_ref[...], mod_ref[3:4, :], mod_ref[4:5, :]).astype(_BF16)
    p = _dot(h, w_ref[...])
    q = p[:, :ATTN_WIDTH]
    kk = p[:, ATTN_WIDTH:ATTN_WIDTH + KV_WIDTH]
    v = p[:, ATTN_WIDTH + KV_WIDTH:ATTN_WIDTH + 2 * KV_WIDTH]
    base = ATTN_WIDTH + 2 * KV_WIDTH
    u_pool = p[:, base:base + POOL_WIDTH]
    u_conv = p[:, base + POOL_WIDTH:base + POOL_WIDTH + CONV_WIDTH]
    bgate = p[:, base + POOL_WIDTH + CONV_WIDTH:base + POOL_WIDTH + 2 * CONV_WIDTH]
    cgate = p[:, base + POOL_WIDTH + 2 * CONV_WIDTH:]

    qn = jnp.concatenate(
        [q[:, j * MXU_COLS:(j + 1) * MXU_COLS]
         * lax.rsqrt(_head_mean_sq(q[:, j * MXU_COLS:(j + 1) * MXU_COLS], seg_ref) + EPS)
         for j in range(ATTN_WIDTH // MXU_COLS)], axis=1) * gq_ref[...]
    seg_k = seg_ref[:KV_WIDTH, :KV_WIDTH]
    kn = kk * lax.rsqrt(_dot((kk * kk).astype(_BF16), seg_k) + EPS) * gk_ref[...]
    if not latent:
        k32_ref[...] = kn
        v32_ref[...] = v
    else:
        cos, slo, shi = cos_ref[...], slo_ref[...], shi_ref[...]
        tile4 = lambda t: jnp.concatenate([t] * (ATTN_WIDTH // KV_WIDTH), axis=1)
        qn = _rotate(qn, tile4(cos), tile4(slo), tile4(shi))
        kn = _rotate(kn, cos, slo, shi)
    q_ref[...] = (qn * (HEAD_DIM ** -0.5)).astype(_BF16)

    low = lax.broadcasted_iota(jnp.int32, kn.shape, 1) < HEAD_DIM

    def dup(t, out_ref):
        sw = pltpu.roll(t, HEAD_DIM, 1)
        out_ref[:, :KV_WIDTH] = jnp.where(low, t, sw).astype(_BF16)
        out_ref[:, KV_WIDTH:] = jnp.where(low, sw, t).astype(_BF16)

    dup(kn, kd_ref)
    dup(v, vd_ref)
    mix_ref[:, :POOL_WIDTH] = u_pool
    mix_ref[:, POOL_WIDTH:POOL_WIDTH + CONV_WIDTH] = cgate * u_conv
    mix_ref[:, POOL_WIDTH + CONV_WIDTH:] = bgate


def _proj(x2d, mods, layer, mod_row_of_tile, g, w, gq, gk, seg, rope, seq_len, tm, latent):
    n_tok = x2d.shape[0]
    resident = functools.partial(pl.BlockSpec, pipeline_mode=pl.Buffered(1))
    const = lambda i: (0, 0)
    row = lambda i: (i, 0)
    in_specs = [
        pl.BlockSpec((tm, D_MODEL), row),
        pl.BlockSpec((None, None, N_MOD, D_MODEL), lambda i: (layer, mod_row_of_tile(i), 0, 0)),
        pl.BlockSpec((1, D_MODEL), const),
        resident((D_MODEL, IN_WIDTH), const),
        pl.BlockSpec((1, ATTN_WIDTH), const),
        pl.BlockSpec((1, KV_WIDTH), const),
        resident((MXU_COLS, MXU_COLS), const),
    ]
    args = [x2d, mods, g, w, gq, gk, seg]
    out_specs = [
        pl.BlockSpec((tm, ATTN_WIDTH), row),
        pl.BlockSpec((tm, 2 * KV_WIDTH), row),
        pl.BlockSpec((tm, 2 * KV_WIDTH), row),
        pl.BlockSpec((tm, MIX_IN_WIDTH), row),
    ]
    out_shape = [
        jax.ShapeDtypeStruct((n_tok, ATTN_WIDTH), _BF16),
        jax.ShapeDtypeStruct((n_tok, 2 * KV_WIDTH), _BF16),
        jax.ShapeDtypeStruct((n_tok, 2 * KV_WIDTH), _BF16),
        jax.ShapeDtypeStruct((n_tok, MIX_IN_WIDTH), _F32),
    ]
    if latent:
        tiles_per_seq = seq_len // tm
        pos = lambda i: (i % tiles_per_seq, 0)
        in_specs += [pl.BlockSpec((tm, KV_WIDTH), pos)] * 3
        args += list(rope)
    else:
        out_specs += [pl.BlockSpec((tm, KV_WIDTH), row)] * 2
        out_shape += [jax.ShapeDtypeStruct((n_tok, KV_WIDTH), _F32)] * 2
    return pl.pallas_call(
        functools.partial(_proj_kernel, latent=latent),
        grid=(n_tok // tm,),
        in_specs=in_specs,
        out_specs=out_specs,
        out_shape=out_shape,
        name="mixer_proj_latent" if latent else "mixer_proj_context",
        compiler_params=pltpu.CompilerParams(
            dimension_semantics=("arbitrary",), vmem_limit_bytes=VMEM_LIMIT),
    )(*args)


def _attend(qs, sink_col, segments):
    scores = []
    for keys, _, mask in segments:
        s = _dot_nt(qs, keys)
        if mask is not None:
            s = jnp.where(mask, s, NEG)
        scores.append(s)
    m = sink_col
    for s in scores:
        m = jnp.maximum(m, jnp.max(s, axis=-1, keepdims=True))
    den = jnp.exp(sink_col - m)
    out = None
    for s, (_, values, _) in zip(scores, segments):
        e = jnp.exp(s - m)
        den = den + jnp.sum(e, axis=-1, keepdims=True)
        pv = _dot(e.astype(_BF16), values)
        out = pv if out is None else out + pv
    return out / den


def _pool_conv(mix_ref, prev_ref, next_ref, first_tile, last_tile, tile_start, seq_len,
               pool_w_ref, pool_scale_ref, conv_w_ref):
    tq = mix_ref.shape[0]
    prev = jnp.where(first_tile, 0.0, prev_ref[...])
    nxt = jnp.where(last_tile, 0.0, next_ref[...])
    cur = mix_ref[...]
    pad = jnp.concatenate([prev, cur, nxt], axis=0)

    up = pad[:, :POOL_WIDTH]
    w2 = up[1:] + up[:-1]
    w4 = w2[:-2] + w2[2:]
    w8 = w4[:-4] + w4[4:]
    w16 = w8[:-8] + w8[8:]
    lane = lax.broadcasted_iota(jnp.int32, (tq, POOL_WIDTH), 1)
    group = lane // POOL_GROUP
    win = jnp.where(group == 0, w2[7:7 + tq],
                    jnp.where(group == 1, w4[6:6 + tq],
                              jnp.where(group == 2, w8[4:4 + tq], w16[:tq])))
    tpos = tile_start + lax.broadcasted_iota(jnp.int32, (tq, POOL_WIDTH), 0)
    half = jnp.left_shift(1, group)
    cnt = (jnp.minimum(tpos + half, seq_len) - jnp.maximum(tpos - half, 0)).astype(_F32)
    u = cur[:, :POOL_WIDTH]
    pooled = (win / cnt - u).astype(_BF16)
    pool = _dot(pooled, pool_w_ref[...]) * pool_scale_ref[...]

    uc = pad[:, POOL_WIDTH:POOL_WIDTH + CONV_WIDTH]
    conv = (uc[HALO - 1:HALO - 1 + tq] * conv_w_ref[0:1, :] + uc[HALO:HALO + tq] * conv_w_ref[1:2, :]
            + uc[HALO + 1:HALO + 1 + tq] * conv_w_ref[2:3, :])
    conv = cur[:, POOL_WIDTH + CONV_WIDTH:] * conv
    return pool, conv


def _mix_kernel(*refs, latent, seq_len):
    if latent:
        (sink_ref, x_ref, mod_ref, q_ref, kd_ref, vd_ref, ckd_ref, cvd_ref, mix_ref, prev_ref, next_ref,
         pool_w_ref, pool_scale_ref, conv_w_ref, w_out_ref, o_ref, y_ref) = refs
    else:
        (sink_ref, x_ref, mod_ref, q_ref, kd_ref, vd_ref, mix_ref, prev_ref, next_ref,
         pool_w_ref, pool_scale_ref, conv_w_ref, w_out_ref, o_ref, y_ref) = refs
    tq = x_ref.shape[0]
    tile = pl.program_id(1)
    tile_start = tile * tq
    lane = lax.broadcasted_iota(jnp.int32, (BLOCK, LANES), 1)
    low = lane < HEAD_DIM
    rows = GQA_GROUP * BLOCK
    local_len = 3 * BLOCK

    for jb in range(tq // BLOCK):
        qb = q_ref[jb * BLOCK:(jb + 1) * BLOCK, :]
        if latent:
            q0 = tile_start + jb * BLOCK
            kstart = pl.multiple_of(jnp.clip(q0 - BLOCK, 0, seq_len - local_len), BLOCK)
            qpos = q0 + (lax.broadcasted_iota(jnp.int32, (rows, local_len), 0) & (BLOCK - 1))
            kpos = kstart + lax.broadcasted_iota(jnp.int32, (rows, local_len), 1)
            mask = jnp.abs(kpos - qpos) <= WINDOW
        heads = []
        for kh in range(N_KV_HEADS):
            parts = []
            for g in range(GQA_GROUP):
                hd = kh * GQA_GROUP + g
                pair = qb[:, (hd // 2) * LANES:(hd // 2 + 1) * LANES]
                keep = low if hd % 2 == 0 else jnp.logical_not(low)
                parts.append(jnp.where(keep, pair, jnp.zeros_like(pair)))
            qs = jnp.concatenate(parts, axis=0)
            sink_col = jnp.concatenate(
                [jnp.full((BLOCK, 1), sink_ref[kh * GQA_GROUP + g], _F32) for g in range(GQA_GROUP)], axis=0)
            ksl = slice(kh * LANES, (kh + 1) * LANES)
            if latent:
                segments = [
                    (kd_ref[pl.ds(kstart, local_len), ksl], vd_ref[pl.ds(kstart, local_len), ksl], mask),
                    (ckd_ref[:, ksl], cvd_ref[:, ksl], None),
                ]
            else:
                segments = [(kd_ref[:, ksl], vd_ref[:, ksl], None)]
            heads.append(_attend(qs, sink_col, segments))
        for pair_idx in range(N_HEADS // 2):
            kh, g0 = divmod(2 * pair_idx, GQA_GROUP)
            o = heads[kh]
            both = jnp.where(low, o[g0 * BLOCK:(g0 + 1) * BLOCK], o[(g0 + 1) * BLOCK:(g0 + 2) * BLOCK])
            y_ref[jb * BLOCK:(jb + 1) * BLOCK, pair_idx * LANES:(pair_idx + 1) * LANES] = both.astype(_BF16)

    pool, conv = _pool_conv(mix_ref, prev_ref, next_ref, tile == 0, tile == pl.num_programs(1) - 1,
                            tile_start, seq_len, pool_w_ref, pool_scale_ref, conv_w_ref)
    y_ref[:, ATTN_WIDTH:ATTN_WIDTH + POOL_WIDTH] = pool.astype(_BF16)
    y_ref[:, ATTN_WIDTH + POOL_WIDTH:] = conv.astype(_BF16)
    y = _dot(y_ref[...], w_out_ref[...])
    o_ref[...] = x_ref[...] + mod_ref[5:6, :] * y


def _mix(x3d, mods, layer, mod_row_of_batch, q, kd, vd, cache, sink, mix_in, pool_w, pool_scale, conv_w, w_out,
         tq, latent):
    n_batch, seq_len, _ = x3d.shape
    n_tiles = seq_len // tq
    halo_blocks = seq_len // HALO
    per_tile = tq // HALO
    resident = functools.partial(pl.BlockSpec, pipeline_mode=pl.Buffered(1))
    const = lambda b, i: (0, 0)
    tile_map = lambda b, i: (b, i, 0)
    seq_map = lambda b, i: (b, 0, 0)
    in_specs = [
        pl.BlockSpec(memory_space=pltpu.SMEM),
        pl.BlockSpec((None, tq, D_MODEL), tile_map),
        pl.BlockSpec((None, None, N_MOD, D_MODEL), lambda b, i: (layer, mod_row_of_batch(b), 0, 0)),
        pl.BlockSpec((None, tq, ATTN_WIDTH), tile_map),
        pl.BlockSpec((None, seq_len, 2 * KV_WIDTH), seq_map),
        pl.BlockSpec((None, seq_len, 2 * KV_WIDTH), seq_map),
    ]
    args = [sink, x3d, mods, q, kd, vd]
    if latent:
        ckd, cvd = cache
        in_specs += [pl.BlockSpec((None,) + ckd.shape[1:], seq_map)] * 2
        args += [ckd, cvd]
    in_specs += [
        pl.BlockSpec((None, tq, MIX_IN_WIDTH), tile_map),
        pl.BlockSpec((None, HALO, MIX_IN_WIDTH), lambda b, i: (b, jnp.maximum(i * per_tile - 1, 0), 0)),
        pl.BlockSpec((None, HALO, MIX_IN_WIDTH),
                     lambda b, i: (b, jnp.minimum((i + 1) * per_tile, halo_blocks - 1), 0)),
        resident((POOL_WIDTH, POOL_WIDTH), const),
        pl.BlockSpec((1, POOL_WIDTH), const),
        pl.BlockSpec((3, CONV_WIDTH), const),
        resident((D_MODEL, D_MODEL), const),
    ]
    args += [mix_in, mix_in, mix_in, pool_w, pool_scale, conv_w, w_out]
    return pl.pallas_call(
        functools.partial(_mix_kernel, latent=latent, seq_len=seq_len),
        grid=(n_batch, n_tiles),
        in_specs=in_specs,
        out_specs=pl.BlockSpec((None, tq, D_MODEL), tile_map),
        out_shape=jax.ShapeDtypeStruct(x3d.shape, _F32),
        scratch_shapes=[pltpu.VMEM((tq, D_MODEL), _BF16)],
        name="mixer_latent" if latent else "mixer_context",
        compiler_params=pltpu.CompilerParams(
            dimension_semantics=("arbitrary", "arbitrary"), vmem_limit_bytes=VMEM_LIMIT),
    )(*args)


def _rope_tables(seq_len):
    rows = seq_len // GRID_W
    row = jnp.repeat(jnp.arange(rows), GRID_W).astype(_F32)
    col = jnp.tile(jnp.arange(GRID_W), rows).astype(_F32)
    half = HEAD_DIM // 2
    inv = ROPE_BASE ** (-jnp.arange(0, half, 2, dtype=_F32) / half)
    zeros = jnp.zeros((seq_len, half // 2), _F32)

    def tables(pos):
        ang = pos[:, None] * inv[None, :]
        cos, sin = jnp.cos(ang), jnp.sin(ang)
        return (jnp.concatenate([cos, cos], axis=1), jnp.concatenate([-sin, zeros], axis=1),
                jnp.concatenate([zeros, sin], axis=1))

    per_head = [jnp.concatenate([a, b], axis=1) for a, b in zip(tables(row), tables(col))]
    return [jnp.concatenate([t, t], axis=1) for t in per_head]


def _dup_heads(t):
    return jnp.concatenate([t[:, :, 0], t[:, :, 0], t[:, :, 1], t[:, :, 1]], axis=-1).astype(_BF16)


def kernel(x_prompt, x_sample, cache_k, cache_v, c, c_ctx, w_ada, b_ada, norm_g, w_ffn_in, w_ffn_out, w_in, w_out,
           q_norm_g, k_norm_g, sink, pool_w, pool_scale, conv_w):
    n_ctx, ctx_len, _ = x_prompt.shape
    n_lat, lat_len, _ = x_sample.shape
    ctx_row = n_lat

    cond = jnp.concatenate([c, c_ctx[None, :], jnp.zeros((COND_ROWS - n_lat - 1, D_MODEL), _F32)], axis=0)
    mods = _ada_rows(cond, w_ada, b_ada).reshape(DEPTH, COND_ROWS, N_MOD, D_MODEL)

    w_ffn_in_b = (w_ffn_in.reshape(DEPTH, 2, D_MODEL, 2, N_FF_CHUNKS, FF_CHUNK)
                  .transpose(0, 1, 2, 4, 3, 5).reshape(DEPTH, 2, D_MODEL, 2 * D_FF).astype(_BF16))
    w_ffn_out_b = w_ffn_out.astype(_BF16)
    w_in_b = w_in.astype(_BF16)
    w_out_b = w_out.astype(_BF16)
    eye = jnp.eye(len(POOL_WINDOWS), dtype=_F32)
    pool_w_b = jnp.einsum('lgcd,gh->lgchd', pool_w, eye).reshape(DEPTH, POOL_WIDTH, POOL_WIDTH).astype(_BF16)
    head_of = jnp.arange(MXU_COLS) // HEAD_DIM
    seg = (jnp.where(head_of[:, None] == head_of[None, :], 1.0 / HEAD_DIM, 0.0)).astype(_BF16)
    gq = jnp.tile(q_norm_g, (1, N_HEADS)).reshape(DEPTH, 1, ATTN_WIDTH)
    gk = jnp.tile(k_norm_g, (1, N_KV_HEADS)).reshape(DEPTH, 1, KV_WIDTH)
    rope = _rope_tables(lat_len)

    tm = 512
    tq_lat = 256
    groups = {
        'ctx': dict(x=x_prompt, latent=False, tq=ctx_len,
                    tile_row=lambda i: ctx_row, batch_row=lambda b: ctx_row),
        'lat': dict(x=x_sample, latent=True, tq=tq_lat,
                    tile_row=lambda i: i // (lat_len // tm), batch_row=lambda b: b),
    }
    new_k, new_v = [], []
    for l in range(DEPTH):
        cache = (_dup_heads(cache_k[:, l]), _dup_heads(cache_v[:, l]))
        for name in ('ctx', 'lat'):
            gr = groups[name]
            x3d = gr['x']
            n_batch, seq_len, _ = x3d.shape
            x2d = x3d.reshape(n_batch * seq_len, D_MODEL)
            x2d = _ffn(x2d, mods, l, gr['tile_row'], 0, norm_g[l, 0:1], w_ffn_in_b[l, 0], w_ffn_out_b[l, 0], tm)
            outs = _proj(x2d, mods, l, gr['tile_row'], norm_g[l, 1:2], w_in_b[l], gq[l], gk[l], seg, rope,
                         seq_len, tm, gr['latent'])
            q, kd, vd, mix_in = outs[:4]
            if not gr['latent']:
                new_k.append(outs[4].reshape(n_batch, seq_len, N_KV_HEADS, HEAD_DIM))
                new_v.append(outs[5].reshape(n_batch, seq_len, N_KV_HEADS, HEAD_DIM))
            to3d = lambda t: t.reshape(n_batch, seq_len, t.shape[-1])
            x3d = _mix(x2d.reshape(x3d.shape), mods, l, gr['batch_row'], to3d(q), to3d(kd), to3d(vd),
                       cache if gr['latent'] else None, sink[l], to3d(mix_in), pool_w_b[l], pool_scale[l][None, :],
                       conv_w[l], w_out_b[l], gr['tq'], gr['latent'])
            x2d = x3d.reshape(n_batch * seq_len, D_MODEL)
            x2d = _ffn(x2d, mods, l, gr['tile_row'], 2, norm_g[l, 2:3], w_ffn_in_b[l, 1], w_ffn_out_b[l, 1], tm)
            gr['x'] = x2d.reshape(x3d.shape)
    return (groups['ctx']['x'], groups['lat']['x'], jnp.stack(new_k, axis=1), jnp.stack(new_v, axis=1))
```

```python
import functools
import math

import numpy as np
import jax
import jax.numpy as jnp
from jax import lax
from jax.experimental import pallas as pl
from jax.experimental.pallas import tpu as pltpu

D_MODEL = 1024
DEPTH = 2
GRID_W = 64
HEAD_DIM = 64
ATTN_WIDTH = 512
N_HEADS = 8
N_KV_HEADS = 2
GQA_GROUP = 4
KV_WIDTH = 128
POOL_WIDTH = 256
POOL_WINDOWS = (2, 4, 8, 16)
POOL_GROUP = 64
CONV_WIDTH = 256
WINDOW = 128
BLOCK = 128
D_FF = 2816
ROPE_BASE = 10000.0
EPS = 1e-6
NEG = -1e30
N_MOD = 9
IN_WIDTH = 1792
MIX_IN_WIDTH = POOL_WIDTH + 2 * CONV_WIDTH
LOG2E = math.log2(math.e)

LANES = 128
HALO = 8
MXU_COLS = 256
FF_CHUNK = MXU_COLS
N_FF_CHUNKS = D_FF // FF_CHUNK
COND_ROWS = 16
ADA_CHUNK = 1152
K_SLAB = 2 * KV_WIDTH
V_SLAB = 2 * (KV_WIDTH + LANES)
LOCAL_LEN = 3 * BLOCK
VMEM_LIMIT = 56 * 1024 * 1024

_BF16 = jnp.bfloat16
_F32 = jnp.float32


def _dot(a, b):
    return jnp.dot(a, b, preferred_element_type=_F32)


def _dot_nt(a, b):
    return lax.dot_general(a, b, (((1,), (1,)), ((), ())), preferred_element_type=_F32)


def _silu(x):
    return x * (1.0 / (1.0 + jnp.exp(-x)))


def _modulated_norm(x, g, shift, scale):
    y = x * lax.rsqrt(jnp.mean(x * x, axis=-1, keepdims=True) + EPS)
    return (y * g) * (1.0 + scale) + shift


def _resident(shape, index_map):
    return pl.BlockSpec(shape, index_map, pipeline_mode=pl.Buffered(1))


def _ada_kernel(cond_ref, w_ref, b_ref, o_ref):
    s = _silu(cond_ref[...]).astype(_BF16)
    o_ref[...] = _dot(s, w_ref[...].astype(_BF16)) + b_ref[...]


def _ada_rows(cond, w_ada, b_ada):
    n_out = N_MOD * D_MODEL
    return pl.pallas_call(
        _ada_kernel,
        grid=(DEPTH, n_out // ADA_CHUNK),
        in_specs=[
            pl.BlockSpec((COND_ROWS, D_MODEL), lambda l, j: (0, 0)),
            pl.BlockSpec((None, D_MODEL, ADA_CHUNK), lambda l, j: (l, 0, j)),
            pl.BlockSpec((None, 1, ADA_CHUNK), lambda l, j: (l, 0, j)),
        ],
        out_specs=pl.BlockSpec((None, COND_ROWS, ADA_CHUNK), lambda l, j: (l, 0, j)),
        out_shape=jax.ShapeDtypeStruct((DEPTH, COND_ROWS, n_out), _F32),
        name="ada_rows",
        compiler_params=pltpu.CompilerParams(
            dimension_semantics=("arbitrary", "arbitrary"), vmem_limit_bytes=VMEM_LIMIT),
    )(cond, w_ada, b_ada.reshape(DEPTH, 1, n_out))


def _ffn_kernel(x_ref, mod_ref, g_ref, w_in_ref, w_out_ref, o_ref, *, k):
    x = x_ref[...]
    shift = mod_ref[3 * k:3 * k + 1, :]
    scale = mod_ref[3 * k + 1:3 * k + 2, :]
    gate = mod_ref[3 * k + 2:3 * k + 3, :]
    h = _modulated_norm(x, g_ref[k:k + 1, :], shift, scale).astype(_BF16)
    acc = jnp.zeros(x.shape, _F32)
    for c in range(N_FF_CHUNKS):
        lo = FF_CHUNK * c
        gt = _dot(h, w_in_ref[:, lo:lo + FF_CHUNK])
        up = _dot(h, w_in_ref[:, D_FF + lo:D_FF + lo + FF_CHUNK])
        a = (_silu(gt) * up).astype(_BF16)
        acc = acc + _dot(a, w_out_ref[lo:lo + FF_CHUNK, :])
    o_ref[...] = x + (0.5 * gate) * acc


def _ffn(x2d, mods, layer, mod_row_of_tile, k, norm_g, w_in, w_out, tm):
    n_tok = x2d.shape[0]
    which = k // 2
    return pl.pallas_call(
        functools.partial(_ffn_kernel, k=k),
        grid=(n_tok // tm,),
        in_specs=[
            pl.BlockSpec((tm, D_MODEL), lambda i: (i, 0)),
            pl.BlockSpec((None, None, N_MOD, D_MODEL), lambda i: (layer, mod_row_of_tile(i), 0, 0)),
            pl.BlockSpec((None, 3, D_MODEL), lambda i: (layer, 0, 0)),
            _resident((None, None, D_MODEL, 2 * D_FF), lambda i: (layer, which, 0, 0)),
            _resident((None, None, D_FF, D_MODEL), lambda i: (layer, which, 0, 0)),
        ],
        out_specs=pl.BlockSpec((tm, D_MODEL), lambda i: (i, 0)),
        out_shape=jax.ShapeDtypeStruct(x2d.shape, _F32),
        name="ffn_half_step",
        compiler_params=pltpu.CompilerParams(
            dimension_semantics=("arbitrary",), vmem_limit_bytes=VMEM_LIMIT),
    )(x2d, mods, norm_g, w_in, w_out)


def _rotate(v, cos, sin_lo, sin_hi):
    n = v.shape[-1]
    quarter = HEAD_DIM // 4
    return v * cos + pltpu.roll(v, n - quarter, 1) * sin_lo + pltpu.roll(v, quarter, 1) * sin_hi


def _proj_rows(rs, refs, latent):
    if latent:
        (x_ref, mod_ref, g_ref, w_ref, gq_ref, gk_ref, seg_ref, cos_ref, slo_ref, shi_ref,
         q_ref, kd_ref, vd_ref, mix_ref) = refs
    else:
        (x_ref, mod_ref, g_ref, w_ref, gq_ref, gk_ref, seg_ref,
         q_ref, kd_ref, vd_ref, mix_ref, k32_ref, v32_ref) = refs
    x = x_ref[rs, :]
    h = _modulated_norm(x, g_ref[1:2, :], mod_ref[3:4, :], mod_ref[4:5, :]).astype(_BF16)
    p = _dot(h, w_ref[...])
    q = p[:, :ATTN_WIDTH]
    kk = p[:, ATTN_WIDTH:ATTN_WIDTH + KV_WIDTH]
    v = p[:, ATTN_WIDTH + KV_WIDTH:ATTN_WIDTH + 2 * KV_WIDTH]
    base = ATTN_WIDTH + 2 * KV_WIDTH
    u_pool = p[:, base:base + POOL_WIDTH]
    u_conv = p[:, base + POOL_WIDTH:base + POOL_WIDTH + CONV_WIDTH]
    bgate = p[:, base + POOL_WIDTH + CONV_WIDTH:base + POOL_WIDTH + 2 * CONV_WIDTH]
    cgate = p[:, base + POOL_WIDTH + 2 * CONV_WIDTH:]

    def head_norm(t, seg):
        return t * lax.rsqrt(_dot((t * t).astype(_BF16), seg) + EPS)

    qn = jnp.concatenate(
        [head_norm(q[:, j * MXU_COLS:(j + 1) * MXU_COLS], seg_ref[...]) for j in range(ATTN_WIDTH // MXU_COLS)],
        axis=1) * gq_ref[...]
    kn = head_norm(kk, seg_ref[:KV_WIDTH, :KV_WIDTH]) * gk_ref[...]
    if not latent:
        k32_ref[rs, :] = kn
        v32_ref[rs, :] = v
    else:
        cos, slo, shi = cos_ref[rs, :], slo_ref[rs, :], shi_ref[rs, :]
        tile4 = lambda t: jnp.concatenate([t] * (ATTN_WIDTH // KV_WIDTH), axis=1)
        qn = _rotate(qn, tile4(cos), tile4(slo), tile4(shi))
        kn = _rotate(kn, cos, slo, shi)
    q_ref[rs, :] = (qn * (HEAD_DIM ** -0.5 * LOG2E)).astype(_BF16)

    low = lax.broadcasted_iota(jnp.int32, kn.shape, 1) < HEAD_DIM
    ones = jnp.ones(kn.shape, _BF16)
    ksw = pltpu.roll(kn, HEAD_DIM, 1)
    kd_ref[rs, :KV_WIDTH] = jnp.where(low, kn, ksw).astype(_BF16)
    kd_ref[rs, KV_WIDTH:] = jnp.where(low, ksw, kn).astype(_BF16)
    vsw = pltpu.roll(v, HEAD_DIM, 1)
    vd_ref[rs, 0 * LANES:1 * LANES] = jnp.where(low, v, vsw).astype(_BF16)
    vd_ref[rs, 1 * LANES:2 * LANES] = ones
    vd_ref[rs, 2 * LANES:3 * LANES] = jnp.where(low, vsw, v).astype(_BF16)
    vd_ref[rs, 3 * LANES:4 * LANES] = ones
    mix_ref[rs, :POOL_WIDTH] = u_pool
    mix_ref[rs, POOL_WIDTH:POOL_WIDTH + CONV_WIDTH] = cgate * u_conv
    mix_ref[rs, POOL_WIDTH + CONV_WIDTH:] = bgate


def _proj_kernel(*refs, latent, sub_rows):
    for r0 in range(0, refs[0].shape[0], sub_rows):
        _proj_rows(slice(r0, r0 + sub_rows), refs, latent)


def _proj(x2d, mods, layer, mod_row_of_tile, norm_g, w, gq, gk, seg, rope, seq_len, tm, latent):
    n_tok = x2d.shape[0]
    const = lambda i: (0, 0)
    row = lambda i: (i, 0)
    per_layer = lambda i: (layer, 0, 0)
    in_specs = [
        pl.BlockSpec((tm, D_MODEL), row),
        pl.BlockSpec((None, None, N_MOD, D_MODEL), lambda i: (layer, mod_row_of_tile(i), 0, 0)),
        pl.BlockSpec((None, 3, D_MODEL), per_layer),
        _resident((None, D_MODEL, IN_WIDTH), per_layer),
        pl.BlockSpec((None, 1, ATTN_WIDTH), per_layer),
        pl.BlockSpec((None, 1, KV_WIDTH), per_layer),
        _resident((MXU_COLS, MXU_COLS), const),
    ]
    args = [x2d, mods, norm_g, w, gq, gk, seg]
    out_specs = [
        pl.BlockSpec((tm, ATTN_WIDTH), row),
        pl.BlockSpec((tm, K_SLAB), row),
        pl.BlockSpec((tm, V_SLAB), row),
        pl.BlockSpec((tm, MIX_IN_WIDTH), row),
    ]
    out_shape = [
        jax.ShapeDtypeStruct((n_tok, ATTN_WIDTH), _BF16),
        jax.ShapeDtypeStruct((n_tok, K_SLAB), _BF16),
        jax.ShapeDtypeStruct((n_tok, V_SLAB), _BF16),
        jax.ShapeDtypeStruct((n_tok, MIX_IN_WIDTH), _F32),
    ]
    if latent:
        tiles_per_seq = seq_len // tm
        pos = lambda i: (i % tiles_per_seq, 0)
        in_specs += [pl.BlockSpec((tm, KV_WIDTH), pos)] * 3
        args += list(rope)
    else:
        out_specs += [pl.BlockSpec((tm, KV_WIDTH), row)] * 2
        out_shape += [jax.ShapeDtypeStruct((n_tok, KV_WIDTH), _F32)] * 2
    return pl.pallas_call(
        functools.partial(_proj_kernel, latent=latent, sub_rows=tm // 2),
        grid=(n_tok // tm,),
        in_specs=in_specs,
        out_specs=out_specs,
        out_shape=out_shape,
        name="mixer_proj_latent" if latent else "mixer_proj_context",
        compiler_params=pltpu.CompilerParams(
            dimension_semantics=("arbitrary",), vmem_limit_bytes=VMEM_LIMIT),
    )(*args)


def _block_row(jb):
    return jb * BLOCK if isinstance(jb, int) else pl.multiple_of(jb * BLOCK, BLOCK)


def _score_stage(jb, kh, io, s_ref, m_ref):
    latent, seq_len, tile_start, low = io['latent'], io['seq_len'], io['tile_start'], io['low']
    row0 = _block_row(jb)
    qb = io['q'][pl.ds(row0, BLOCK), kh * GQA_GROUP * HEAD_DIM:(kh + 1) * GQA_GROUP * HEAD_DIM]
    parts = []
    for g in range(GQA_GROUP):
        pair = qb[:, (g // 2) * LANES:(g // 2 + 1) * LANES]
        keep = low if g % 2 == 0 else jnp.logical_not(low)
        parts.append(jnp.where(keep, pair, jnp.zeros_like(pair)))
    qs = jnp.concatenate(parts, axis=0)
    ksl = slice(kh * KV_WIDTH, (kh + 1) * KV_WIDTH)
    if latent:
        q0 = tile_start + row0
        kstart = pl.multiple_of(jnp.clip(q0 - BLOCK, 0, seq_len - LOCAL_LEN), BLOCK)
        variant = jnp.where(q0 == 0, 0, jnp.where(q0 == seq_len - BLOCK, 2, 1))
        s_loc = _dot_nt(qs, io['kd'][pl.ds(kstart, LOCAL_LEN), ksl])
        s_loc = (s_loc.reshape(GQA_GROUP, BLOCK, LOCAL_LEN) + io['bias'][variant][None]).reshape(s_loc.shape)
        pieces = [s_loc, _dot_nt(qs, io['ckd'][:, ksl])]
    else:
        pieces = [_dot_nt(qs, io['kd'][:, ksl])]
    m = jnp.concatenate(
        [jnp.full((BLOCK, 1), io['sink'][kh * GQA_GROUP + g] * LOG2E, _F32) for g in range(GQA_GROUP)], axis=0)
    off = 0
    for piece in pieces:
        m = jnp.maximum(m, jnp.max(piece, axis=-1, keepdims=True))
        s_ref[:, off:off + piece.shape[1]] = piece
        off += piece.shape[1]
    m_ref[...] = jnp.broadcast_to(m, m_ref.shape)


def _exp_stage(kh, io, s_ref, m_ref, p_ref, e_ref):
    m_b = m_ref[...]
    for c in range(s_ref.shape[1] // LANES):
        cols = slice(c * LANES, (c + 1) * LANES)
        p_ref[:, cols] = jnp.exp2(s_ref[:, cols] - m_b).astype(_BF16)
    sink_b = jnp.concatenate(
        [jnp.full((BLOCK, LANES), io['sink'][kh * GQA_GROUP + g] * LOG2E, _F32) for g in range(GQA_GROUP)], axis=0)
    e_ref[...] = jnp.exp2(sink_b - m_b)


def _value_stage(jb, kh, io, p_ref, e_ref):
    latent, seq_len, tile_start, low = io['latent'], io['seq_len'], io['tile_start'], io['low']
    row0 = _block_row(jb)
    vsl = slice(kh * (V_SLAB // 2), (kh + 1) * (V_SLAB // 2))
    if latent:
        kstart = pl.multiple_of(jnp.clip(tile_start + row0 - BLOCK, 0, seq_len - LOCAL_LEN), BLOCK)
        acc = (_dot(p_ref[:, :LOCAL_LEN], io['vd'][pl.ds(kstart, LOCAL_LEN), vsl])
               + _dot(p_ref[:, LOCAL_LEN:], io['cvd'][:, vsl]))
    else:
        acc = _dot(p_ref[...], io['vd'][:, vsl])
    den = acc[:, LANES:] + e_ref[...]
    o = acc[:, :LANES] / den
    for j in range(GQA_GROUP // 2):
        both = jnp.where(low, o[2 * j * BLOCK:(2 * j + 1) * BLOCK], o[(2 * j + 1) * BLOCK:(2 * j + 2) * BLOCK])
        col = (kh * (GQA_GROUP // 2) + j) * LANES
        io['y'][pl.ds(row0, BLOCK), col:col + LANES] = both.astype(_BF16)


def _pool_conv(mix_ref, prev_ref, next_ref, first_tile, last_tile, tile_start, seq_len,
               pool_w_ref, pool_scale_ref, conv_w_ref):
    tq = mix_ref.shape[0]
    prev = jnp.where(first_tile, 0.0, prev_ref[...])
    nxt = jnp.where(last_tile, 0.0, next_ref[...])
    cur = mix_ref[...]
    pad = jnp.concatenate([prev, cur, nxt], axis=0)

    up = pad[:, :POOL_WIDTH]
    w2 = up[1:] + up[:-1]
    w4 = w2[:-2] + w2[2:]
    w8 = w4[:-4] + w4[4:]
    w16 = w8[:-8] + w8[8:]
    lane = lax.broadcasted_iota(jnp.int32, (tq, POOL_WIDTH), 1)
    group = lane // POOL_GROUP
    win = jnp.where(group == 0, w2[7:7 + tq],
                    jnp.where(group == 1, w4[6:6 + tq],
                              jnp.where(group == 2, w8[4:4 + tq], w16[:tq])))
    tpos = tile_start + lax.broadcasted_iota(jnp.int32, (tq, POOL_WIDTH), 0)
    half = jnp.left_shift(1, group)
    cnt = (jnp.minimum(tpos + half, seq_len) - jnp.maximum(tpos - half, 0)).astype(_F32)
    u = cur[:, :POOL_WIDTH]
    pooled = (win / cnt - u).astype(_BF16)
    pool = _dot(pooled, pool_w_ref[...]) * pool_scale_ref[...]

    uc = pad[:, POOL_WIDTH:POOL_WIDTH + CONV_WIDTH]
    conv = (uc[HALO - 1:HALO - 1 + tq] * conv_w_ref[0:1, :] + uc[HALO:HALO + tq] * conv_w_ref[1:2, :]
            + uc[HALO + 1:HALO + 1 + tq] * conv_w_ref[2:3, :])
    conv = cur[:, POOL_WIDTH + CONV_WIDTH:] * conv
    return pool, conv


def _mix_kernel(*refs, latent, seq_len):
    if latent:
        (sink_ref, x_ref, mod_ref, q_ref, kd_ref, vd_ref, ckd_ref, cvd_ref, bias_ref, mix_ref, prev_ref, next_ref,
         pool_w_ref, pool_scale_ref, conv_w_ref, w_out_ref, o_ref, y_ref, s0_ref, s1_ref, m0_ref, m1_ref, p0_ref, p1_ref, e0_ref, e1_ref) = refs
    else:
        (sink_ref, x_ref, mod_ref, q_ref, kd_ref, vd_ref, mix_ref, prev_ref, next_ref,
         pool_w_ref, pool_scale_ref, conv_w_ref, w_out_ref, o_ref, y_ref, s0_ref, s1_ref, m0_ref, m1_ref, p0_ref, p1_ref, e0_ref, e1_ref) = refs
        ckd_ref = cvd_ref = bias_ref = None
    tq = x_ref.shape[0]
    n_blocks = tq // BLOCK
    tile = pl.program_id(1)
    tile_start = tile * tq
    io = dict(latent=latent, seq_len=seq_len, tile_start=tile_start, sink=sink_ref, q=q_ref, kd=kd_ref, vd=vd_ref,
              ckd=ckd_ref, cvd=cvd_ref, bias=bias_ref, y=y_ref,
              low=lax.broadcasted_iota(jnp.int32, (BLOCK, LANES), 1) < HEAD_DIM)

    bufs = ((s0_ref, m0_ref, p0_ref, e0_ref), (s1_ref, m1_ref, p1_ref, e1_ref))
    score = lambda jb, kh: _score_stage(jb, kh, io, bufs[kh][0], bufs[kh][1])
    expo = lambda kh: _exp_stage(kh, io, *bufs[kh])
    value = lambda jb, kh: _value_stage(jb, kh, io, bufs[kh][2], bufs[kh][3])

    score(0, 0)
    expo(0)
    score(0, 1)

    def body(jb, carry):
        score(jb + 1, 0)
        expo(1)
        value(jb, 0)
        score(jb + 1, 1)
        expo(0)
        value(jb, 1)
        return carry

    lax.fori_loop(0, n_blocks - 1, body, 0, unroll=True)
    value(n_blocks - 1, 0)
    expo(1)
    value(n_blocks - 1, 1)

    pool, conv = _pool_conv(mix_ref, prev_ref, next_ref, tile == 0, tile == pl.num_programs(1) - 1,
                            tile_start, seq_len, pool_w_ref, pool_scale_ref, conv_w_ref)
    y_ref[:, ATTN_WIDTH:ATTN_WIDTH + POOL_WIDTH] = pool.astype(_BF16)
    y_ref[:, ATTN_WIDTH + POOL_WIDTH:] = conv.astype(_BF16)
    y = _dot(y_ref[...], w_out_ref[...])
    o_ref[...] = x_ref[...] + mod_ref[5:6, :] * y


def _mix(x3d, mods, layer, mod_row_of_batch, q, kd, vd, cache, bias, sink, mix_in, pool_w, pool_scale, conv_w, w_out,
         tq, latent):
    n_batch, seq_len, _ = x3d.shape
    n_tiles = seq_len // tq
    halo_blocks = seq_len // HALO
    per_tile = tq // HALO
    tile_map = lambda b, i: (b, i, 0)
    seq_map = lambda b, i: (b, 0, 0)
    per_layer = lambda b, i: (layer, 0, 0)
    n_keys = LOCAL_LEN + cache[0].shape[2] if latent else seq_len
    in_specs = [
        pl.BlockSpec(memory_space=pltpu.SMEM),
        pl.BlockSpec((None, tq, D_MODEL), tile_map),
        pl.BlockSpec((None, None, N_MOD, D_MODEL), lambda b, i: (layer, mod_row_of_batch(b), 0, 0)),
        pl.BlockSpec((None, tq, ATTN_WIDTH), tile_map),
        pl.BlockSpec((None, seq_len, K_SLAB), seq_map),
        pl.BlockSpec((None, seq_len, V_SLAB), seq_map),
    ]
    args = [sink, x3d, mods, q, kd, vd]
    if latent:
        ckd, cvd = cache
        in_specs += [
            pl.BlockSpec((None, None) + ckd.shape[2:], lambda b, i: (layer, b, 0, 0)),
            pl.BlockSpec((None, None) + cvd.shape[2:], lambda b, i: (layer, b, 0, 0)),
            _resident(bias.shape, lambda b, i: (0, 0, 0)),
        ]
        args += [ckd, cvd, bias]
    in_specs += [
        pl.BlockSpec((None, tq, MIX_IN_WIDTH), tile_map),
        pl.BlockSpec((None, HALO, MIX_IN_WIDTH), lambda b, i: (b, jnp.maximum(i * per_tile - 1, 0), 0)),
        pl.BlockSpec((None, HALO, MIX_IN_WIDTH),
                     lambda b, i: (b, jnp.minimum((i + 1) * per_tile, halo_blocks - 1), 0)),
        _resident((None, POOL_WIDTH, POOL_WIDTH), per_layer),
        pl.BlockSpec((None, 1, POOL_WIDTH), per_layer),
        pl.BlockSpec((None, 3, CONV_WIDTH), per_layer),
        _resident((None, D_MODEL, D_MODEL), per_layer),
    ]
    args += [mix_in, mix_in, mix_in, pool_w, pool_scale, conv_w, w_out]
    return pl.pallas_call(
        functools.partial(_mix_kernel, latent=latent, seq_len=seq_len),
        grid=(n_batch, n_tiles),
        in_specs=in_specs,
        out_specs=pl.BlockSpec((None, tq, D_MODEL), tile_map),
        out_shape=jax.ShapeDtypeStruct(x3d.shape, _F32),
        scratch_shapes=[pltpu.VMEM((tq, D_MODEL), _BF16)]
        + [pltpu.VMEM((GQA_GROUP * BLOCK, n_keys), _F32)] * 2 + [pltpu.VMEM((GQA_GROUP * BLOCK, LANES), _F32)] * 2
        + [pltpu.VMEM((GQA_GROUP * BLOCK, n_keys), _BF16)] * 2 + [pltpu.VMEM((GQA_GROUP * BLOCK, LANES), _F32)] * 2,
        name="mixer_latent" if latent else "mixer_context",
        compiler_params=pltpu.CompilerParams(
            dimension_semantics=("arbitrary", "arbitrary"), vmem_limit_bytes=VMEM_LIMIT),
    )(*args)


def _rope_tables(seq_len):
    rows = seq_len // GRID_W
    row = np.repeat(np.arange(rows), GRID_W).astype(np.float32)
    col = np.tile(np.arange(GRID_W), rows).astype(np.float32)
    half = HEAD_DIM // 2
    inv = (np.float32(ROPE_BASE) ** (-np.arange(0, half, 2, dtype=np.float32) / np.float32(half))).astype(np.float32)
    zeros = np.zeros((seq_len, half // 2), np.float32)

    def tables(pos):
        ang = (pos[:, None] * inv[None, :]).astype(np.float32).astype(np.float64)
        cos, sin = np.cos(ang).astype(np.float32), np.sin(ang).astype(np.float32)
        return (np.concatenate([cos, cos], axis=1), np.concatenate([-sin, zeros], axis=1),
                np.concatenate([zeros, sin], axis=1))

    per_head = [np.concatenate([a, b], axis=1) for a, b in zip(tables(row), tables(col))]
    return [jnp.asarray(np.concatenate([t, t], axis=1)) for t in per_head]


def _window_bias():
    r = np.arange(BLOCK)[None, :, None]
    c = np.arange(LOCAL_LEN)[None, None, :]
    v = np.arange(3)[:, None, None]
    return jnp.asarray(np.where(np.abs(c - BLOCK * v - r) <= WINDOW, 0.0, NEG).astype(np.float32))


def _key_slabs(t):
    t = jnp.swapaxes(t, 0, 1)
    return jnp.concatenate([t[..., 0, :], t[..., 0, :], t[..., 1, :], t[..., 1, :]], axis=-1).astype(_BF16)


def _value_slabs(t):
    t = jnp.swapaxes(t, 0, 1)
    ones = jnp.ones(t.shape[:3] + (LANES,), t.dtype)
    return jnp.concatenate([t[..., 0, :], t[..., 0, :], ones, t[..., 1, :], t[..., 1, :], ones],
                           axis=-1).astype(_BF16)


def kernel(x_prompt, x_sample, cache_k, cache_v, c, c_ctx, w_ada, b_ada, norm_g, w_ffn_in, w_ffn_out, w_in, w_out,
           q_norm_g, k_norm_g, sink, pool_w, pool_scale, conv_w):
    n_ctx, ctx_len, _ = x_prompt.shape
    n_lat, lat_len, _ = x_sample.shape
    ctx_row = n_lat

    cond = jnp.concatenate([c, c_ctx[None, :], jnp.zeros((COND_ROWS - n_lat - 1, D_MODEL), _F32)], axis=0)
    mods = _ada_rows(cond, w_ada, b_ada).reshape(DEPTH, COND_ROWS, N_MOD, D_MODEL)

    w_ffn_in_b = w_ffn_in.astype(_BF16)
    w_ffn_out_b = w_ffn_out.astype(_BF16)
    w_in_b = w_in.astype(_BF16)
    w_out_b = w_out.astype(_BF16)
    eye = jnp.eye(len(POOL_WINDOWS), dtype=_F32)
    pool_w_b = jnp.einsum('lgcd,gh->lgchd', pool_w, eye).reshape(DEPTH, POOL_WIDTH, POOL_WIDTH).astype(_BF16)
    head_of = np.arange(MXU_COLS) // HEAD_DIM
    seg = jnp.asarray(np.where(head_of[:, None] == head_of[None, :], 1.0 / HEAD_DIM, 0.0), _BF16)
    gq = jnp.tile(q_norm_g, (1, N_HEADS)).reshape(DEPTH, 1, ATTN_WIDTH)
    gk = jnp.tile(k_norm_g, (1, N_KV_HEADS)).reshape(DEPTH, 1, KV_WIDTH)
    pool_scale3 = pool_scale.reshape(DEPTH, 1, POOL_WIDTH)
    rope = _rope_tables(lat_len)
    bias = _window_bias()
    cache = (_key_slabs(cache_k), _value_slabs(cache_v))

    tm = 512
    tq_lat = 1024
    groups = {
        'ctx': dict(x=x_prompt, latent=False, tq=ctx_len,
                    tile_row=lambda i: ctx_row, batch_row=lambda b: ctx_row),
        'lat': dict(x=x_sample, latent=True, tq=tq_lat,
                    tile_row=lambda i: i // (lat_len // tm), batch_row=lambda b: b),
    }
    new_k, new_v = [], []
    for l in range(DEPTH):
        for name in ('ctx', 'lat'):
            gr = groups[name]
            x3d = gr['x']
            n_batch, seq_len, _ = x3d.shape
            x2d = x3d.reshape(n_batch * seq_len, D_MODEL)
            x2d = _ffn(x2d, mods, l, gr['tile_row'], 0, norm_g, w_ffn_in_b, w_ffn_out_b, tm)
            outs = _proj(x2d, mods, l, gr['tile_row'], norm_g, w_in_b, gq, gk, seg, rope, seq_len, tm, gr['latent'])
            q, kd, vd, mix_in = outs[:4]
            if not gr['latent']:
                new_k.append(outs[4].reshape(n_batch, seq_len, N_KV_HEADS, HEAD_DIM))
                new_v.append(outs[5].reshape(n_batch, seq_len, N_KV_HEADS, HEAD_DIM))
            to3d = lambda t: t.reshape(n_batch, seq_len, t.shape[-1])
            x3d = _mix(x2d.reshape(x3d.shape), mods, l, gr['batch_row'], to3d(q), to3d(kd), to3d(vd),
                       cache if gr['latent'] else None, bias, sink[l], to3d(mix_in), pool_w_b, pool_scale3,
                       conv_w, w_out_b, gr['tq'], gr['latent'])
            x2d = x3d.reshape(n_batch * seq_len, D_MODEL)
            x2d = _ffn(x2d, mods, l, gr['tile_row'], 2, norm_g, w_ffn_in_b, w_ffn_out_b, tm)
            gr['x'] = x2d.reshape(x3d.shape)
    return (groups['ctx']['x'], groups['lat']['x'], jnp.stack(new_k, axis=1), jnp.stack(new_v, axis=1))
```

```python
import functools
import math

import numpy as np
import jax
import jax.numpy as jnp
from jax import lax
from jax.experimental import pallas as pl
from jax.experimental.pallas import tpu as pltpu

D_MODEL = 1024
DEPTH = 2
GRID_W = 64
HEAD_DIM = 64
ATTN_WIDTH = 512
N_HEADS = 8
N_KV_HEADS = 2
GQA_GROUP = 4
KV_WIDTH = 128
POOL_WIDTH = 256
POOL_WINDOWS = (2, 4, 8, 16)
POOL_GROUP = 64
CONV_WIDTH = 256
WINDOW = 128
BLOCK = 128
D_FF = 2816
ROPE_BASE = 10000.0
EPS = 1e-6
NEG = -1e30
N_MOD = 9
IN_WIDTH = 1792
MIX_IN_WIDTH = POOL_WIDTH + 2 * CONV_WIDTH
LOG2E = math.log2(math.e)

LANES = 128
HALO = 8
MXU_COLS = 256
FF_CHUNK = MXU_COLS
N_FF_CHUNKS = D_FF // FF_CHUNK
COND_ROWS = 16
ADA_CHUNK = 1152
K_SLAB = 2 * KV_WIDTH
V_SLAB = 2 * (KV_WIDTH + LANES)
LOCAL_LEN = 3 * BLOCK
VMEM_LIMIT = 56 * 1024 * 1024

_BF16 = jnp.bfloat16
_F32 = jnp.float32


def _dot(a, b):
    return jnp.dot(a, b, preferred_element_type=_F32)


def _dot_nt(a, b):
    return lax.dot_general(a, b, (((1,), (1,)), ((), ())), preferred_element_type=_F32)


def _silu(x):
    return x * (1.0 / (1.0 + jnp.exp(-x)))


def _modulated_norm(x, g, shift, scale):
    y = x * lax.rsqrt(jnp.mean(x * x, axis=-1, keepdims=True) + EPS)
    return (y * g) * (1.0 + scale) + shift


def _resident(shape, index_map):
    return pl.BlockSpec(shape, index_map, pipeline_mode=pl.Buffered(1))


def _ada_kernel(cond_ref, w_ref, b_ref, o_ref):
    s = _silu(cond_ref[...]).astype(_BF16)
    o_ref[...] = _dot(s, w_ref[...].astype(_BF16)) + b_ref[...]


def _ada_rows(cond, w_ada, b_ada):
    n_out = N_MOD * D_MODEL
    return pl.pallas_call(
        _ada_kernel,
        grid=(DEPTH, n_out // ADA_CHUNK),
        in_specs=[
            pl.BlockSpec((COND_ROWS, D_MODEL), lambda l, j: (0, 0)),
            pl.BlockSpec((None, D_MODEL, ADA_CHUNK), lambda l, j: (l, 0, j)),
            pl.BlockSpec((None, 1, ADA_CHUNK), lambda l, j: (l, 0, j)),
        ],
        out_specs=pl.BlockSpec((None, COND_ROWS, ADA_CHUNK), lambda l, j: (l, 0, j)),
        out_shape=jax.ShapeDtypeStruct((DEPTH, COND_ROWS, n_out), _F32),
        name="ada_rows",
        compiler_params=pltpu.CompilerParams(
            dimension_semantics=("arbitrary", "arbitrary"), vmem_limit_bytes=VMEM_LIMIT),
    )(cond, w_ada, b_ada.reshape(DEPTH, 1, n_out))


def _ffn_kernel(x_ref, mod_ref, g_ref, w_in_ref, w_out_ref, o_ref, *, k):
    x = x_ref[...]
    shift = mod_ref[3 * k:3 * k + 1, :]
    scale = mod_ref[3 * k + 1:3 * k + 2, :]
    gate = mod_ref[3 * k + 2:3 * k + 3, :]
    h = _modulated_norm(x, g_ref[k:k + 1, :], shift, scale).astype(_BF16)
    acc = jnp.zeros(x.shape, _F32)
    for c in range(N_FF_CHUNKS):
        lo = FF_CHUNK * c
        gt = _dot(h, w_in_ref[:, lo:lo + FF_CHUNK])
        up = _dot(h, w_in_ref[:, D_FF + lo:D_FF + lo + FF_CHUNK])
        a = (_silu(gt) * up).astype(_BF16)
        acc = acc + _dot(a, w_out_ref[lo:lo + FF_CHUNK, :])
    o_ref[...] = x + (0.5 * gate) * acc


def _ffn(x2d, mods, layer, mod_row_of_tile, k, norm_g, w_in, w_out, tm):
    n_tok = x2d.shape[0]
    which = k // 2
    return pl.pallas_call(
        functools.partial(_ffn_kernel, k=k),
        grid=(n_tok // tm,),
        in_specs=[
            pl.BlockSpec((tm, D_MODEL), lambda i: (i, 0)),
            pl.BlockSpec((None, None, N_MOD, D_MODEL), lambda i: (layer, mod_row_of_tile(i), 0, 0)),
            pl.BlockSpec((None, 3, D_MODEL), lambda i: (layer, 0, 0)),
            _resident((None, None, D_MODEL, 2 * D_FF), lambda i: (layer, which, 0, 0)),
            _resident((None, None, D_FF, D_MODEL), lambda i: (layer, which, 0, 0)),
        ],
        out_specs=pl.BlockSpec((tm, D_MODEL), lambda i: (i, 0)),
        out_shape=jax.ShapeDtypeStruct(x2d.shape, _F32),
        name="ffn_half_step",
        compiler_params=pltpu.CompilerParams(
            dimension_semantics=("arbitrary",), vmem_limit_bytes=VMEM_LIMIT),
    )(x2d, mods, norm_g, w_in, w_out)


def _rotate(v, cos, sin_lo, sin_hi):
    n = v.shape[-1]
    quarter = HEAD_DIM // 4
    return v * cos + pltpu.roll(v, n - quarter, 1) * sin_lo + pltpu.roll(v, quarter, 1) * sin_hi


def _proj_rows(rs, refs, latent):
    if latent:
        (x_ref, mod_ref, g_ref, w_ref, gq_ref, gk_ref, seg_ref, cos_ref, slo_ref, shi_ref,
         q_ref, kd_ref, vd_ref, mix_ref) = refs
    else:
        (x_ref, mod_ref, g_ref, w_ref, gq_ref, gk_ref, seg_ref,
         q_ref, kd_ref, vd_ref, mix_ref, k32_ref, v32_ref) = refs
    x = x_ref[rs, :]
    h = _modulated_norm(x, g_ref[1:2, :], mod_ref[3:4, :], mod_ref[4:5, :]).astype(_BF16)
    p = _dot(h, w_ref[...])
    q = p[:, :ATTN_WIDTH]
    kk = p[:, ATTN_WIDTH:ATTN_WIDTH + KV_WIDTH]
    v = p[:, ATTN_WIDTH + KV_WIDTH:ATTN_WIDTH + 2 * KV_WIDTH]
    base = ATTN_WIDTH + 2 * KV_WIDTH
    u_pool = p[:, base:base + POOL_WIDTH]
    u_conv = p[:, base + POOL_WIDTH:base + POOL_WIDTH + CONV_WIDTH]
    bgate = p[:, base + POOL_WIDTH + CONV_WIDTH:base + POOL_WIDTH + 2 * CONV_WIDTH]
    cgate = p[:, base + POOL_WIDTH + 2 * CONV_WIDTH:]

    def head_norm(t, seg):
        return t * lax.rsqrt(_dot((t * t).astype(_BF16), seg) + EPS)

    qn = jnp.concatenate(
        [head_norm(q[:, j * MXU_COLS:(j + 1) * MXU_COLS], seg_ref[...]) for j in range(ATTN_WIDTH // MXU_COLS)],
        axis=1) * gq_ref[...]
    kn = head_norm(kk, seg_ref[:KV_WIDTH, :KV_WIDTH]) * gk_ref[...]
    if not latent:
        k32_ref[rs, :] = kn
        v32_ref[rs, :] = v
    else:
        cos, slo, shi = cos_ref[rs, :], slo_ref[rs, :], shi_ref[rs, :]
        tile4 = lambda t: jnp.concatenate([t] * (ATTN_WIDTH // KV_WIDTH), axis=1)
        qn = _rotate(qn, tile4(cos), tile4(slo), tile4(shi))
        kn = _rotate(kn, cos, slo, shi)
    q_ref[rs, :] = (qn * (HEAD_DIM ** -0.5 * LOG2E)).astype(_BF16)

    low = lax.broadcasted_iota(jnp.int32, kn.shape, 1) < HEAD_DIM
    ones = jnp.ones(kn.shape, _BF16)
    ksw = pltpu.roll(kn, HEAD_DIM, 1)
    kd_ref[rs, :KV_WIDTH] = jnp.where(low, kn, ksw).astype(_BF16)
    kd_ref[rs, KV_WIDTH:] = jnp.where(low, ksw, kn).astype(_BF16)
    vsw = pltpu.roll(v, HEAD_DIM, 1)
    vd_ref[rs, 0 * LANES:1 * LANES] = jnp.where(low, v, vsw).astype(_BF16)
    vd_ref[rs, 1 * LANES:2 * LANES] = ones
    vd_ref[rs, 2 * LANES:3 * LANES] = jnp.where(low, vsw, v).astype(_BF16)
    vd_ref[rs, 3 * LANES:4 * LANES] = ones
    mix_ref[rs, :POOL_WIDTH] = u_pool
    mix_ref[rs, POOL_WIDTH:POOL_WIDTH + CONV_WIDTH] = cgate * u_conv
    mix_ref[rs, POOL_WIDTH + CONV_WIDTH:] = bgate


def _proj_kernel(*refs, latent, sub_rows):
    for r0 in range(0, refs[0].shape[0], sub_rows):
        _proj_rows(slice(r0, r0 + sub_rows), refs, latent)


def _proj(x2d, mods, layer, mod_row_of_tile, norm_g, w, gq, gk, seg, rope, seq_len, tm, latent):
    n_tok = x2d.shape[0]
    const = lambda i: (0, 0)
    row = lambda i: (i, 0)
    per_layer = lambda i: (layer, 0, 0)
    in_specs = [
        pl.BlockSpec((tm, D_MODEL), row),
        pl.BlockSpec((None, None, N_MOD, D_MODEL), lambda i: (layer, mod_row_of_tile(i), 0, 0)),
        pl.BlockSpec((None, 3, D_MODEL), per_layer),
        _resident((None, D_MODEL, IN_WIDTH), per_layer),
        pl.BlockSpec((None, 1, ATTN_WIDTH), per_layer),
        pl.BlockSpec((None, 1, KV_WIDTH), per_layer),
        _resident((MXU_COLS, MXU_COLS), const),
    ]
    args = [x2d, mods, norm_g, w, gq, gk, seg]
    out_specs = [
        pl.BlockSpec((tm, ATTN_WIDTH), row),
        pl.BlockSpec((tm, K_SLAB), row),
        pl.BlockSpec((tm, V_SLAB), row),
        pl.BlockSpec((tm, MIX_IN_WIDTH), row),
    ]
    out_shape = [
        jax.ShapeDtypeStruct((n_tok, ATTN_WIDTH), _BF16),
        jax.ShapeDtypeStruct((n_tok, K_SLAB), _BF16),
        jax.ShapeDtypeStruct((n_tok, V_SLAB), _BF16),
        jax.ShapeDtypeStruct((n_tok, MIX_IN_WIDTH), _F32),
    ]
    if latent:
        tiles_per_seq = seq_len // tm
        pos = lambda i: (i % tiles_per_seq, 0)
        in_specs += [pl.BlockSpec((tm, KV_WIDTH), pos)] * 3
        args += list(rope)
    else:
        out_specs += [pl.BlockSpec((tm, KV_WIDTH), row)] * 2
        out_shape += [jax.ShapeDtypeStruct((n_tok, KV_WIDTH), _F32)] * 2
    return pl.pallas_call(
        functools.partial(_proj_kernel, latent=latent, sub_rows=tm // 2),
        grid=(n_tok // tm,),
        in_specs=in_specs,
        out_specs=out_specs,
        out_shape=out_shape,
        name="mixer_proj_latent" if latent else "mixer_proj_context",
        compiler_params=pltpu.CompilerParams(
            dimension_semantics=("arbitrary",), vmem_limit_bytes=VMEM_LIMIT),
    )(*args)


def _block_row(jb):
    return jb * BLOCK if isinstance(jb, int) else pl.multiple_of(jb * BLOCK, BLOCK)


def _score_stage(jb, kh, io, s_ref, m_ref):
    latent, seq_len, tile_start, low = io['latent'], io['seq_len'], io['tile_start'], io['low']
    row0 = _block_row(jb)
    qb = io['q'][pl.ds(row0, BLOCK), kh * GQA_GROUP * HEAD_DIM:(kh + 1) * GQA_GROUP * HEAD_DIM]
    parts = []
    for g in range(GQA_GROUP):
        pair = qb[:, (g // 2) * LANES:(g // 2 + 1) * LANES]
        keep = low if g % 2 == 0 else jnp.logical_not(low)
        parts.append(jnp.where(keep, pair, jnp.zeros_like(pair)))
    qs = jnp.concatenate(parts, axis=0)
    ksl = slice(kh * KV_WIDTH, (kh + 1) * KV_WIDTH)
    if latent:
        q0 = tile_start + row0
        kstart = pl.multiple_of(jnp.clip(q0 - BLOCK, 0, seq_len - LOCAL_LEN), BLOCK)
        variant = jnp.where(q0 == 0, 0, jnp.where(q0 == seq_len - BLOCK, 2, 1))
        s_loc = _dot_nt(qs, io['kd'][pl.ds(kstart, LOCAL_LEN), ksl])
        s_loc = (s_loc.reshape(GQA_GROUP, BLOCK, LOCAL_LEN) + io['bias'][variant][None]).reshape(s_loc.shape)
        pieces = [s_loc, _dot_nt(qs, io['ckd'][:, ksl])]
    else:
        pieces = [_dot_nt(qs, io['kd'][:, ksl])]
    m = jnp.concatenate(
        [jnp.full((BLOCK, 1), io['sink'][kh * GQA_GROUP + g] * LOG2E, _F32) for g in range(GQA_GROUP)], axis=0)
    off = 0
    for piece in pieces:
        m = jnp.maximum(m, jnp.max(piece, axis=-1, keepdims=True))
        s_ref[:, off:off + piece.shape[1]] = piece
        off += piece.shape[1]
    m_ref[...] = jnp.broadcast_to(m, m_ref.shape)


def _exp_stage(kh, io, s_ref, m_ref, p_ref, e_ref):
    m_b = m_ref[...]
    for c in range(s_ref.shape[1] // LANES):
        cols = slice(c * LANES, (c + 1) * LANES)
        p_ref[:, cols] = jnp.exp2(s_ref[:, cols] - m_b).astype(_BF16)
    sink_b = jnp.concatenate(
        [jnp.full((BLOCK, LANES), io['sink'][kh * GQA_GROUP + g] * LOG2E, _F32) for g in range(GQA_GROUP)], axis=0)
    e_ref[...] = jnp.exp2(sink_b - m_b)


def _value_stage(jb, kh, io, p_ref, e_ref):
    latent, seq_len, tile_start, low = io['latent'], io['seq_len'], io['tile_start'], io['low']
    row0 = _block_row(jb)
    vsl = slice(kh * (V_SLAB // 2), (kh + 1) * (V_SLAB // 2))
    if latent:
        kstart = pl.multiple_of(jnp.clip(tile_start + row0 - BLOCK, 0, seq_len - LOCAL_LEN), BLOCK)
        acc = (_dot(p_ref[:, :LOCAL_LEN], io['vd'][pl.ds(kstart, LOCAL_LEN), vsl])
               + _dot(p_ref[:, LOCAL_LEN:], io['cvd'][:, vsl]))
    else:
        acc = _dot(p_ref[...], io['vd'][:, vsl])
    den = acc[:, LANES:] + e_ref[...]
    o = acc[:, :LANES] / den
    for j in range(GQA_GROUP // 2):
        both = jnp.where(low, o[2 * j * BLOCK:(2 * j + 1) * BLOCK], o[(2 * j + 1) * BLOCK:(2 * j + 2) * BLOCK])
        col = (kh * (GQA_GROUP // 2) + j) * LANES
        io['y'][pl.ds(row0, BLOCK), col:col + LANES] = both.astype(_BF16)


def _pool_conv(mix_ref, prev_ref, next_ref, first_tile, last_tile, tile_start, seq_len,
               pool_w_ref, pool_scale_ref, conv_w_ref):
    tq = mix_ref.shape[0]
    prev = jnp.where(first_tile, 0.0, prev_ref[...])
    nxt = jnp.where(last_tile, 0.0, next_ref[...])
    cur = mix_ref[...]
    pad = jnp.concatenate([prev, cur, nxt], axis=0)

    up = pad[:, :POOL_WIDTH]
    w2 = up[1:] + up[:-1]
    w4 = w2[:-2] + w2[2:]
    w8 = w4[:-4] + w4[4:]
    w16 = w8[:-8] + w8[8:]
    lane = lax.broadcasted_iota(jnp.int32, (tq, POOL_WIDTH), 1)
    group = lane // POOL_GROUP
    win = jnp.where(group == 0, w2[7:7 + tq],
                    jnp.where(group == 1, w4[6:6 + tq],
                              jnp.where(group == 2, w8[4:4 + tq], w16[:tq])))
    tpos = tile_start + lax.broadcasted_iota(jnp.int32, (tq, POOL_WIDTH), 0)
    half = jnp.left_shift(1, group)
    cnt = (jnp.minimum(tpos + half, seq_len) - jnp.maximum(tpos - half, 0)).astype(_F32)
    u = cur[:, :POOL_WIDTH]
    pooled = (win / cnt - u).astype(_BF16)
    pool = _dot(pooled, pool_w_ref[...]) * pool_scale_ref[...]

    uc = pad[:, POOL_WIDTH:POOL_WIDTH + CONV_WIDTH]
    conv = (uc[HALO - 1:HALO - 1 + tq] * conv_w_ref[0:1, :] + uc[HALO:HALO + tq] * conv_w_ref[1:2, :]
            + uc[HALO + 1:HALO + 1 + tq] * conv_w_ref[2:3, :])
    conv = cur[:, POOL_WIDTH + CONV_WIDTH:] * conv
    return pool, conv


def _mix_kernel(*refs, latent, seq_len):
    if latent:
        (sink_ref, x_ref, mod_ref, q_ref, kd_ref, vd_ref, ckd_ref, cvd_ref, bias_ref, mix_ref, prev_ref, next_ref,
         pool_w_ref, pool_scale_ref, conv_w_ref, w_out_ref, o_ref, y_ref, s0_ref, s1_ref, m0_ref, m1_ref, p0_ref, p1_ref, e0_ref, e1_ref) = refs
    else:
        (sink_ref, x_ref, mod_ref, q_ref, kd_ref, vd_ref, mix_ref, prev_ref, next_ref,
         pool_w_ref, pool_scale_ref, conv_w_ref, w_out_ref, o_ref, y_ref, s0_ref, s1_ref, m0_ref, m1_ref, p0_ref, p1_ref, e0_ref, e1_ref) = refs
        ckd_ref = cvd_ref = bias_ref = None
    tq = x_ref.shape[0]
    n_blocks = tq // BLOCK
    tile = pl.program_id(1)
    tile_start = tile * tq
    io = dict(latent=latent, seq_len=seq_len, tile_start=tile_start, sink=sink_ref, q=q_ref, kd=kd_ref, vd=vd_ref,
              ckd=ckd_ref, cvd=cvd_ref, bias=bias_ref, y=y_ref,
              low=lax.broadcasted_iota(jnp.int32, (BLOCK, LANES), 1) < HEAD_DIM)

    bufs = ((s0_ref, m0_ref, p0_ref, e0_ref), (s1_ref, m1_ref, p1_ref, e1_ref))
    score = lambda jb, kh: _score_stage(jb, kh, io, bufs[kh][0], bufs[kh][1])
    expo = lambda kh: _exp_stage(kh, io, *bufs[kh])
    value = lambda jb, kh: _value_stage(jb, kh, io, bufs[kh][2], bufs[kh][3])

    score(0, 0)
    expo(0)
    score(0, 1)

    def body(jb, carry):
        score(jb + 1, 0)
        expo(1)
        value(jb, 0)
        score(jb + 1, 1)
        expo(0)
        value(jb, 1)
        return carry

    lax.fori_loop(0, n_blocks - 1, body, 0, unroll=True)
    value(n_blocks - 1, 0)
    expo(1)
    value(n_blocks - 1, 1)

    pool, conv = _pool_conv(mix_ref, prev_ref, next_ref, tile == 0, tile == pl.num_programs(1) - 1,
                            tile_start, seq_len, pool_w_ref, pool_scale_ref, conv_w_ref)
    y_ref[:, ATTN_WIDTH:ATTN_WIDTH + POOL_WIDTH] = pool.astype(_BF16)
    y_ref[:, ATTN_WIDTH + POOL_WIDTH:] = conv.astype(_BF16)
    y = _dot(y_ref[...], w_out_ref[...])
    o_ref[...] = x_ref[...] + mod_ref[5:6, :] * y


def _mix(x3d, mods, layer, mod_row_of_batch, q, kd, vd, cache, bias, sink, mix_in, pool_w, pool_scale, conv_w, w_out,
         tq, latent):
    n_batch, seq_len, _ = x3d.shape
    n_tiles = seq_len // tq
    halo_blocks = seq_len // HALO
    per_tile = tq // HALO
    tile_map = lambda b, i: (b, i, 0)
    seq_map = lambda b, i: (b, 0, 0)
    per_layer = lambda b, i: (layer, 0, 0)
    n_keys = LOCAL_LEN + cache[0].shape[2] if latent else seq_len
    in_specs = [
        pl.BlockSpec(memory_space=pltpu.SMEM),
        pl.BlockSpec((None, tq, D_MODEL), tile_map),
        pl.BlockSpec((None, None, N_MOD, D_MODEL), lambda b, i: (layer, mod_row_of_batch(b), 0, 0)),
        pl.BlockSpec((None, tq, ATTN_WIDTH), tile_map),
        pl.BlockSpec((None, seq_len, K_SLAB), seq_map),
        pl.BlockSpec((None, seq_len, V_SLAB), seq_map),
    ]
    args = [sink, x3d, mods, q, kd, vd]
    if latent:
        ckd, cvd = cache
        in_specs += [
            pl.BlockSpec((None, None) + ckd.shape[2:], lambda b, i: (layer, b, 0, 0)),
            pl.BlockSpec((None, None) + cvd.shape[2:], lambda b, i: (layer, b, 0, 0)),
            _resident(bias.shape, lambda b, i: (0, 0, 0)),
        ]
        args += [ckd, cvd, bias]
    in_specs += [
        pl.BlockSpec((None, tq, MIX_IN_WIDTH), tile_map),
        pl.BlockSpec((None, HALO, MIX_IN_WIDTH), lambda b, i: (b, jnp.maximum(i * per_tile - 1, 0), 0)),
        pl.BlockSpec((None, HALO, MIX_IN_WIDTH),
                     lambda b, i: (b, jnp.minimum((i + 1) * per_tile, halo_blocks - 1), 0)),
        _resident((None, POOL_WIDTH, POOL_WIDTH), per_layer),
        pl.BlockSpec((None, 1, POOL_WIDTH), per_layer),
        pl.BlockSpec((None, 3, CONV_WIDTH), per_layer),
        _resident((None, D_MODEL, D_MODEL), per_layer),
    ]
    args += [mix_in, mix_in, mix_in, pool_w, pool_scale, conv_w, w_out]
    return pl.pallas_call(
        functools.partial(_mix_kernel, latent=latent, seq_len=seq_len),
        grid=(n_batch, n_tiles),
        in_specs=in_specs,
        out_specs=pl.BlockSpec((None, tq, D_MODEL), tile_map),
        out_shape=jax.ShapeDtypeStruct(x3d.shape, _F32),
        scratch_shapes=[pltpu.VMEM((tq, D_MODEL), _BF16)]
        + [pltpu.VMEM((GQA_GROUP * BLOCK, n_keys), _F32)] * 2 + [pltpu.VMEM((GQA_GROUP * BLOCK, LANES), _F32)] * 2
        + [pltpu.VMEM((GQA_GROUP * BLOCK, n_keys), _BF16)] * 2 + [pltpu.VMEM((GQA_GROUP * BLOCK, LANES), _F32)] * 2,
        name="mixer_latent" if latent else "mixer_context",
        compiler_params=pltpu.CompilerParams(
            dimension_semantics=("arbitrary", "arbitrary"), vmem_limit_bytes=VMEM_LIMIT),
    )(*args)


def _rope_tables(seq_len):
    rows = seq_len // GRID_W
    row = np.repeat(np.arange(rows), GRID_W).astype(np.float32)
    col = np.tile(np.arange(GRID_W), rows).astype(np.float32)
    half = HEAD_DIM // 2
    inv = (np.float32(ROPE_BASE) ** (-np.arange(0, half, 2, dtype=np.float32) / np.float32(half))).astype(np.float32)
    zeros = np.zeros((seq_len, half // 2), np.float32)

    def tables(pos):
        ang = (pos[:, None] * inv[None, :]).astype(np.float32).astype(np.float64)
        cos, sin = np.cos(ang).astype(np.float32), np.sin(ang).astype(np.float32)
        return (np.concatenate([cos, cos], axis=1), np.concatenate([-sin, zeros], axis=1),
                np.concatenate([zeros, sin], axis=1))

    per_head = [np.concatenate([a, b], axis=1) for a, b in zip(tables(row), tables(col))]
    return [jnp.asarray(np.concatenate([t, t], axis=1)) for t in per_head]


def _window_bias():
    r = np.arange(BLOCK)[None, :, None]
    c = np.arange(LOCAL_LEN)[None, None, :]
    v = np.arange(3)[:, None, None]
    return jnp.asarray(np.where(np.abs(c - BLOCK * v - r) <= WINDOW, 0.0, NEG).astype(np.float32))


def _key_slabs(t):
    t = jnp.swapaxes(t, 0, 1)
    return jnp.concatenate([t[..., 0, :], t[..., 0, :], t[..., 1, :], t[..., 1, :]], axis=-1).astype(_BF16)


def _value_slabs(t):
    t = jnp.swapaxes(t, 0, 1)
    ones = jnp.ones(t.shape[:3] + (LANES,), t.dtype)
    return jnp.concatenate([t[..., 0, :], t[..., 0, :], ones, t[..., 1, :], t[..., 1, :], ones],
                           axis=-1).astype(_BF16)


def kernel(x_prompt, x_sample, cache_k, cache_v, c, c_ctx, w_ada, b_ada, norm_g, w_ffn_in, w_ffn_out, w_in, w_out,
           q_norm_g, k_norm_g, sink, pool_w, pool_scale, conv_w):
    n_ctx, ctx_len, _ = x_prompt.shape
    n_lat, lat_len, _ = x_sample.shape
    ctx_row = n_lat

    cond = jnp.concatenate([c, c_ctx[None, :], jnp.zeros((COND_ROWS - n_lat - 1, D_MODEL), _F32)], axis=0)
    mods = _ada_rows(cond, w_ada, b_ada).reshape(DEPTH, COND_ROWS, N_MOD, D_MODEL)

    w_ffn_in_b = w_ffn_in.astype(_BF16)
    w_ffn_out_b = w_ffn_out.astype(_BF16)
    w_in_b = w_in.astype(_BF16)
    w_out_b = w_out.astype(_BF16)
    eye = jnp.eye(len(POOL_WINDOWS), dtype=_F32)
    pool_w_b = jnp.einsum('lgcd,gh->lgchd', pool_w, eye).reshape(DEPTH, POOL_WIDTH, POOL_WIDTH).astype(_BF16)
    head_of = np.arange(MXU_COLS) // HEAD_DIM
    seg = jnp.asarray(np.where(head_of[:, None] == head_of[None, :], 1.0 / HEAD_DIM, 0.0), _BF16)
    gq = jnp.tile(q_norm_g, (1, N_HEADS)).reshape(DEPTH, 1, ATTN_WIDTH)
    gk = jnp.tile(k_norm_g, (1, N_KV_HEADS)).reshape(DEPTH, 1, KV_WIDTH)
    pool_scale3 = pool_scale.reshape(DEPTH, 1, POOL_WIDTH)
    rope = _rope_tables(lat_len)
    bias = _window_bias()
    cache = (_key_slabs(cache_k), _value_slabs(cache_v))

    tm = 1024
    tq_lat = 1024
    groups = {
        'ctx': dict(x=x_prompt, latent=False, tq=ctx_len,
                    tile_row=lambda i: ctx_row, batch_row=lambda b: ctx_row),
        'lat': dict(x=x_sample, latent=True, tq=tq_lat,
                    tile_row=lambda i: i // (lat_len // tm), batch_row=lambda b: b),
    }
    new_k, new_v = [], []
    for l in range(DEPTH):
        for name in ('ctx', 'lat'):
            gr = groups[name]
            x3d = gr['x']
            n_batch, seq_len, _ = x3d.shape
            x2d = x3d.reshape(n_batch * seq_len, D_MODEL)
            x2d = _ffn(x2d, mods, l, gr['tile_row'], 0, norm_g, w_ffn_in_b, w_ffn_out_b, tm)
            outs = _proj(x2d, mods, l, gr['tile_row'], norm_g, w_in_b, gq, gk, seg, rope, seq_len, tm, gr['latent'])
            q, kd, vd, mix_in = outs[:4]
            if not gr['latent']:
                new_k.append(outs[4].reshape(n_batch, seq_len, N_KV_HEADS, HEAD_DIM))
                new_v.append(outs[5].reshape(n_batch, seq_len, N_KV_HEADS, HEAD_DIM))
            to3d = lambda t: t.reshape(n_batch, seq_len, t.shape[-1])
            x3d = _mix(x2d.reshape(x3d.shape), mods, l, gr['batch_row'], to3d(q), to3d(kd), to3d(vd),
                       cache if gr['latent'] else None, bias, sink[l], to3d(mix_in), pool_w_b, pool_scale3,
                       conv_w, w_out_b, gr['tq'], gr['latent'])
            x2d = x3d.reshape(n_batch * seq_len, D_MODEL)
            x2d = _ffn(x2d, mods, l, gr['tile_row'], 2, norm_g, w_ffn_in_b, w_ffn_out_b, tm)
            gr['x'] = x2d.reshape(x3d.shape)
    return (groups['ctx']['x'], groups['lat']['x'], jnp.stack(new_k, axis=1), jnp.stack(new_v, axis=1))
```

```python
import functools
import math

import numpy as np
import jax
import jax.numpy as jnp
from jax import lax
from jax.experimental import pallas as pl
from jax.experimental.pallas import tpu as pltpu

D_MODEL = 1024
DEPTH = 2
GRID_W = 64
HEAD_DIM = 64
ATTN_WIDTH = 512
N_HEADS = 8
N_KV_HEADS = 2
GQA_GROUP = 4
KV_WIDTH = 128
POOL_WIDTH = 256
POOL_WINDOWS = (2, 4, 8, 16)
POOL_GROUP = 64
CONV_WIDTH = 256
WINDOW = 128
BLOCK = 128
D_FF = 2816
ROPE_BASE = 10000.0
EPS = 1e-6
NEG = -1e30
N_MOD = 9
IN_WIDTH = 1792
MIX_IN_WIDTH = POOL_WIDTH + 2 * CONV_WIDTH
LOG2E = math.log2(math.e)

LANES = 128
HALO = 8
MXU_COLS = 256
FF_CHUNK = MXU_COLS
N_FF_CHUNKS = D_FF // FF_CHUNK
COND_ROWS = 16
ADA_CHUNK = 1152
K_SLAB = 2 * KV_WIDTH
V_SLAB = 2 * LANES
LOCAL_LEN = 3 * BLOCK
VMEM_LIMIT = 56 * 1024 * 1024

_BF16 = jnp.bfloat16
_F32 = jnp.float32


def _dot(a, b):
    return jnp.dot(a, b, preferred_element_type=_F32)


def _dot_nt(a, b):
    return lax.dot_general(a, b, (((1,), (1,)), ((), ())), preferred_element_type=_F32)


def _dot_tn(a, b):
    return lax.dot_general(a, b, (((0,), (0,)), ((), ())), preferred_element_type=_F32)


def _silu(x):
    return x * (1.0 / (1.0 + jnp.exp(-x)))


def _modulated_norm(x, g, shift, scale):
    y = x * lax.rsqrt(jnp.mean(x * x, axis=-1, keepdims=True) + EPS)
    return (y * g) * (1.0 + scale) + shift


def _resident(shape, index_map):
    return pl.BlockSpec(shape, index_map, pipeline_mode=pl.Buffered(1))


def _ada_kernel(cond_ref, w_ref, b_ref, o_ref):
    s = _silu(cond_ref[...]).astype(_BF16)
    o_ref[...] = _dot(s, w_ref[...].astype(_BF16)) + b_ref[...]


def _ada_rows(cond, w_ada, b_ada):
    n_out = N_MOD * D_MODEL
    return pl.pallas_call(
        _ada_kernel,
        grid=(DEPTH, n_out // ADA_CHUNK),
        in_specs=[
            pl.BlockSpec((COND_ROWS, D_MODEL), lambda l, j: (0, 0)),
            pl.BlockSpec((None, D_MODEL, ADA_CHUNK), lambda l, j: (l, 0, j)),
            pl.BlockSpec((None, 1, ADA_CHUNK), lambda l, j: (l, 0, j)),
        ],
        out_specs=pl.BlockSpec((None, COND_ROWS, ADA_CHUNK), lambda l, j: (l, 0, j)),
        out_shape=jax.ShapeDtypeStruct((DEPTH, COND_ROWS, n_out), _F32),
        name="ada_rows",
        compiler_params=pltpu.CompilerParams(
            dimension_semantics=("arbitrary", "arbitrary"), vmem_limit_bytes=VMEM_LIMIT),
    )(cond, w_ada, b_ada.reshape(DEPTH, 1, n_out))


def _ffn_kernel(x_ref, mod_ref, g_ref, w_in_ref, w_out_ref, o_ref, *, k):
    x = x_ref[...]
    shift = mod_ref[3 * k:3 * k + 1, :]
    scale = mod_ref[3 * k + 1:3 * k + 2, :]
    gate = mod_ref[3 * k + 2:3 * k + 3, :]
    h = _modulated_norm(x, g_ref[k:k + 1, :], shift, scale).astype(_BF16)
    acc = jnp.zeros(x.shape, _F32)
    for c in range(N_FF_CHUNKS):
        lo = FF_CHUNK * c
        gt = _dot(h, w_in_ref[:, lo:lo + FF_CHUNK])
        up = _dot(h, w_in_ref[:, D_FF + lo:D_FF + lo + FF_CHUNK])
        a = (_silu(gt) * up).astype(_BF16)
        acc = acc + _dot(a, w_out_ref[lo:lo + FF_CHUNK, :])
    o_ref[...] = x + (0.5 * gate) * acc


def _ffn(x2d, mods, layer, mod_row_of_tile, k, norm_g, w_in, w_out, tm):
    n_tok = x2d.shape[0]
    which = k // 2
    return pl.pallas_call(
        functools.partial(_ffn_kernel, k=k),
        grid=(n_tok // tm,),
        in_specs=[
            pl.BlockSpec((tm, D_MODEL), lambda i: (i, 0)),
            pl.BlockSpec((None, None, N_MOD, D_MODEL), lambda i: (layer, mod_row_of_tile(i), 0, 0)),
            pl.BlockSpec((None, 3, D_MODEL), lambda i: (layer, 0, 0)),
            _resident((None, None, D_MODEL, 2 * D_FF), lambda i: (layer, which, 0, 0)),
            _resident((None, None, D_FF, D_MODEL), lambda i: (layer, which, 0, 0)),
        ],
        out_specs=pl.BlockSpec((tm, D_MODEL), lambda i: (i, 0)),
        out_shape=jax.ShapeDtypeStruct(x2d.shape, _F32),
        name="ffn_half_step",
        compiler_params=pltpu.CompilerParams(
            dimension_semantics=("arbitrary",), vmem_limit_bytes=VMEM_LIMIT),
    )(x2d, mods, norm_g, w_in, w_out)


def _rotate(v, cos, sin_lo, sin_hi):
    n = v.shape[-1]
    quarter = HEAD_DIM // 4
    return v * cos + pltpu.roll(v, n - quarter, 1) * sin_lo + pltpu.roll(v, quarter, 1) * sin_hi


def _proj_rows(rs, refs, latent):
    if latent:
        (x_ref, mod_ref, g_ref, w_ref, gq_ref, gk_ref, seg_ref, cos_ref, slo_ref, shi_ref,
         q_ref, kd_ref, vd_ref, mix_ref) = refs
    else:
        (x_ref, mod_ref, g_ref, w_ref, gq_ref, gk_ref, seg_ref,
         q_ref, kd_ref, vd_ref, mix_ref, k32_ref, v32_ref) = refs
    x = x_ref[rs, :]
    h = _modulated_norm(x, g_ref[1:2, :], mod_ref[3:4, :], mod_ref[4:5, :]).astype(_BF16)
    p = _dot(h, w_ref[...])
    q = p[:, :ATTN_WIDTH]
    kk = p[:, ATTN_WIDTH:ATTN_WIDTH + KV_WIDTH]
    v = p[:, ATTN_WIDTH + KV_WIDTH:ATTN_WIDTH + 2 * KV_WIDTH]
    base = ATTN_WIDTH + 2 * KV_WIDTH
    u_pool = p[:, base:base + POOL_WIDTH]
    u_conv = p[:, base + POOL_WIDTH:base + POOL_WIDTH + CONV_WIDTH]
    bgate = p[:, base + POOL_WIDTH + CONV_WIDTH:base + POOL_WIDTH + 2 * CONV_WIDTH]
    cgate = p[:, base + POOL_WIDTH + 2 * CONV_WIDTH:]

    def head_norm(t, seg):
        return t * lax.rsqrt(_dot((t * t).astype(_BF16), seg) + EPS)

    qn = jnp.concatenate(
        [head_norm(q[:, j * MXU_COLS:(j + 1) * MXU_COLS], seg_ref[...]) for j in range(ATTN_WIDTH // MXU_COLS)],
        axis=1) * gq_ref[...]
    kn = head_norm(kk, seg_ref[:KV_WIDTH, :KV_WIDTH]) * gk_ref[...]
    if not latent:
        k32_ref[rs, :] = kn
        v32_ref[rs, :] = v
    else:
        cos, slo, shi = cos_ref[rs, :], slo_ref[rs, :], shi_ref[rs, :]
        tile4 = lambda t: jnp.concatenate([t] * (ATTN_WIDTH // KV_WIDTH), axis=1)
        qn = _rotate(qn, tile4(cos), tile4(slo), tile4(shi))
        kn = _rotate(kn, cos, slo, shi)
    q_ref[rs, :] = (qn * (HEAD_DIM ** -0.5 * LOG2E)).astype(_BF16)

    low = lax.broadcasted_iota(jnp.int32, kn.shape, 1) < HEAD_DIM
    ksw = pltpu.roll(kn, HEAD_DIM, 1)
    kd_ref[rs, :KV_WIDTH] = jnp.where(low, kn, ksw).astype(_BF16)
    kd_ref[rs, KV_WIDTH:] = jnp.where(low, ksw, kn).astype(_BF16)
    vd_ref[rs, :LANES] = jnp.where(low, v, 1.0).astype(_BF16)
    vd_ref[rs, LANES:] = jnp.where(low, pltpu.roll(v, HEAD_DIM, 1), 1.0).astype(_BF16)
    mix_ref[rs, :POOL_WIDTH] = u_pool
    mix_ref[rs, POOL_WIDTH:POOL_WIDTH + CONV_WIDTH] = cgate * u_conv
    mix_ref[rs, POOL_WIDTH + CONV_WIDTH:] = bgate


def _proj_kernel(*refs, latent, sub_rows):
    for r0 in range(0, refs[0].shape[0], sub_rows):
        _proj_rows(slice(r0, r0 + sub_rows), refs, latent)


def _proj(x2d, mods, layer, mod_row_of_tile, norm_g, w, gq, gk, seg, rope, seq_len, tm, latent):
    n_tok = x2d.shape[0]
    const = lambda i: (0, 0)
    row = lambda i: (i, 0)
    per_layer = lambda i: (layer, 0, 0)
    in_specs = [
        pl.BlockSpec((tm, D_MODEL), row),
        pl.BlockSpec((None, None, N_MOD, D_MODEL), lambda i: (layer, mod_row_of_tile(i), 0, 0)),
        pl.BlockSpec((None, 3, D_MODEL), per_layer),
        _resident((None, D_MODEL, IN_WIDTH), per_layer),
        pl.BlockSpec((None, 1, ATTN_WIDTH), per_layer),
        pl.BlockSpec((None, 1, KV_WIDTH), per_layer),
        _resident((MXU_COLS, MXU_COLS), const),
    ]
    args = [x2d, mods, norm_g, w, gq, gk, seg]
    out_specs = [
        pl.BlockSpec((tm, ATTN_WIDTH), row),
        pl.BlockSpec((tm, K_SLAB), row),
        pl.BlockSpec((tm, V_SLAB), row),
        pl.BlockSpec((tm, MIX_IN_WIDTH), row),
    ]
    out_shape = [
        jax.ShapeDtypeStruct((n_tok, ATTN_WIDTH), _BF16),
        jax.ShapeDtypeStruct((n_tok, K_SLAB), _BF16),
        jax.ShapeDtypeStruct((n_tok, V_SLAB), _BF16),
        jax.ShapeDtypeStruct((n_tok, MIX_IN_WIDTH), _F32),
    ]
    if latent:
        tiles_per_seq = seq_len // tm
        pos = lambda i: (i % tiles_per_seq, 0)
        in_specs += [pl.BlockSpec((tm, KV_WIDTH), pos)] * 3
        args += list(rope)
    else:
        out_specs += [pl.BlockSpec((tm, KV_WIDTH), row)] * 2
        out_shape += [jax.ShapeDtypeStruct((n_tok, KV_WIDTH), _F32)] * 2
    return pl.pallas_call(
        functools.partial(_proj_kernel, latent=latent, sub_rows=tm // 2),
        grid=(n_tok // tm,),
        in_specs=in_specs,
        out_specs=out_specs,
        out_shape=out_shape,
        name="mixer_proj_latent" if latent else "mixer_proj_context",
        compiler_params=pltpu.CompilerParams(
            dimension_semantics=("arbitrary",), vmem_limit_bytes=VMEM_LIMIT),
    )(*args)


def _block_row(jb):
    return jb * BLOCK if isinstance(jb, int) else pl.multiple_of(jb * BLOCK, BLOCK)


def _sink_row(io, kh):
    return jnp.concatenate(
        [jnp.full((1, BLOCK), io['sink'][kh * GQA_GROUP + g] * LOG2E, _F32) for g in range(GQA_GROUP)], axis=1)


def _score_stage(jb, kh, io, s_ref, m_ref):
    latent, seq_len, tile_start, low = io['latent'], io['seq_len'], io['tile_start'], io['low']
    row0 = _block_row(jb)
    qb = io['q'][pl.ds(row0, BLOCK), kh * GQA_GROUP * HEAD_DIM:(kh + 1) * GQA_GROUP * HEAD_DIM]
    parts = []
    for g in range(GQA_GROUP):
        pair = qb[:, (g // 2) * LANES:(g // 2 + 1) * LANES]
        keep = low if g % 2 == 0 else jnp.logical_not(low)
        parts.append(jnp.where(keep, pair, jnp.zeros_like(pair)))
    qs = jnp.concatenate(parts, axis=0)
    ksl = slice(kh * KV_WIDTH, (kh + 1) * KV_WIDTH)
    if latent:
        q0 = tile_start + row0
        kstart = pl.multiple_of(jnp.clip(q0 - BLOCK, 0, seq_len - LOCAL_LEN), BLOCK)
        variant = jnp.where(q0 == 0, 0, jnp.where(q0 == seq_len - BLOCK, 2, 1))
        bias = io['bias'][variant]
        s_loc = _dot_nt(io['kd'][pl.ds(kstart, LOCAL_LEN), ksl], qs) + jnp.concatenate([bias] * GQA_GROUP, axis=1)
        pieces = [s_loc, _dot_nt(io['ckd'][:, ksl], qs)]
    else:
        pieces = [_dot_nt(io['kd'][:, ksl], qs)]
    m = _sink_row(io, kh)
    off = 0
    for piece in pieces:
        m = jnp.maximum(m, jnp.max(piece, axis=0, keepdims=True))
        s_ref[off:off + piece.shape[0], :] = piece
        off += piece.shape[0]
    m_ref[...] = jnp.broadcast_to(m, m_ref.shape)


def _exp_stage(kh, io, s_ref, m_ref, p_ref, e_ref):
    m = m_ref[0:1, :]
    for r0 in range(0, s_ref.shape[0], BLOCK):
        p_ref[r0:r0 + BLOCK, :] = jnp.exp2(s_ref[r0:r0 + BLOCK, :] - m).astype(_BF16)
    e_ref[...] = jnp.broadcast_to(jnp.exp2(_sink_row(io, kh) - m), e_ref.shape)


def _value_stage(jb, kh, io, p_ref, e_ref):
    latent, seq_len, tile_start = io['latent'], io['seq_len'], io['tile_start']
    row0 = _block_row(jb)
    vsl = slice(kh * LANES, (kh + 1) * LANES)
    if latent:
        kstart = pl.multiple_of(jnp.clip(tile_start + row0 - BLOCK, 0, seq_len - LOCAL_LEN), BLOCK)
        acc = (_dot_tn(io['vd'][pl.ds(kstart, LOCAL_LEN), vsl], p_ref[:LOCAL_LEN, :])
               + _dot_tn(io['cvd'][:, vsl], p_ref[LOCAL_LEN:, :]))
    else:
        acc = _dot_tn(io['vd'][:, vsl], p_ref[...])
    den = acc[HEAD_DIM:HEAD_DIM + 1, :] + e_ref[0:1, :]
    o = acc[:HEAD_DIM, :] / den
    for j in range(GQA_GROUP // 2):
        pair = jnp.concatenate([o[:, 2 * j * BLOCK:(2 * j + 1) * BLOCK], o[:, (2 * j + 1) * BLOCK:(2 * j + 2) * BLOCK]],
                               axis=0)
        col = (kh * (GQA_GROUP // 2) + j) * LANES
        io['y'][pl.ds(row0, BLOCK), col:col + LANES] = pair.T.astype(_BF16)


def _pool_conv(mix_ref, prev_ref, next_ref, first_tile, last_tile, tile_start, seq_len,
               pool_w_ref, pool_scale_ref, conv_w_ref):
    tq = mix_ref.shape[0]
    prev = jnp.where(first_tile, 0.0, prev_ref[...])
    nxt = jnp.where(last_tile, 0.0, next_ref[...])
    cur = mix_ref[...]
    pad = jnp.concatenate([prev, cur, nxt], axis=0)

    up = pad[:, :POOL_WIDTH]
    w2 = up[1:] + up[:-1]
    w4 = w2[:-2] + w2[2:]
    w8 = w4[:-4] + w4[4:]
    w16 = w8[:-8] + w8[8:]
    lane = lax.broadcasted_iota(jnp.int32, (tq, POOL_WIDTH), 1)
    group = lane // POOL_GROUP
    win = jnp.where(group == 0, w2[7:7 + tq],
                    jnp.where(group == 1, w4[6:6 + tq],
                              jnp.where(group == 2, w8[4:4 + tq], w16[:tq])))
    tpos = tile_start + lax.broadcasted_iota(jnp.int32, (tq, POOL_WIDTH), 0)
    half = jnp.left_shift(1, group)
    cnt = (jnp.minimum(tpos + half, seq_len) - jnp.maximum(tpos - half, 0)).astype(_F32)
    u = cur[:, :POOL_WIDTH]
    pooled = (win / cnt - u).astype(_BF16)
    pool = _dot(pooled, pool_w_ref[...]) * pool_scale_ref[...]

    uc = pad[:, POOL_WIDTH:POOL_WIDTH + CONV_WIDTH]
    conv = (uc[HALO - 1:HALO - 1 + tq] * conv_w_ref[0:1, :] + uc[HALO:HALO + tq] * conv_w_ref[1:2, :]
            + uc[HALO + 1:HALO + 1 + tq] * conv_w_ref[2:3, :])
    conv = cur[:, POOL_WIDTH + CONV_WIDTH:] * conv
    return pool, conv


def _mix_kernel(*refs, latent, seq_len):
    if latent:
        (sink_ref, x_ref, mod_ref, q_ref, kd_ref, vd_ref, ckd_ref, cvd_ref, bias_ref, mix_ref, prev_ref, next_ref,
         pool_w_ref, pool_scale_ref, conv_w_ref, w_out_ref, o_ref, y_ref, s0_ref, s1_ref, m0_ref, m1_ref, p0_ref, p1_ref, e0_ref, e1_ref) = refs
    else:
        (sink_ref, x_ref, mod_ref, q_ref, kd_ref, vd_ref, mix_ref, prev_ref, next_ref,
         pool_w_ref, pool_scale_ref, conv_w_ref, w_out_ref, o_ref, y_ref, s0_ref, s1_ref, m0_ref, m1_ref, p0_ref, p1_ref, e0_ref, e1_ref) = refs
        ckd_ref = cvd_ref = bias_ref = None
    tq = x_ref.shape[0]
    n_blocks = tq // BLOCK
    tile = pl.program_id(1)
    tile_start = tile * tq
    io = dict(latent=latent, seq_len=seq_len, tile_start=tile_start, sink=sink_ref, q=q_ref, kd=kd_ref, vd=vd_ref,
              ckd=ckd_ref, cvd=cvd_ref, bias=bias_ref, y=y_ref,
              low=lax.broadcasted_iota(jnp.int32, (BLOCK, LANES), 1) < HEAD_DIM)

    bufs = ((s0_ref, m0_ref, p0_ref, e0_ref), (s1_ref, m1_ref, p1_ref, e1_ref))
    score = lambda jb, kh: _score_stage(jb, kh, io, bufs[kh][0], bufs[kh][1])
    expo = lambda kh: _exp_stage(kh, io, *bufs[kh])
    value = lambda jb, kh: _value_stage(jb, kh, io, bufs[kh][2], bufs[kh][3])

    score(0, 0)
    expo(0)
    score(0, 1)

    def body(jb, carry):
        score(jb + 1, 0)
        expo(1)
        value(jb, 0)
        score(jb + 1, 1)
        expo(0)
        value(jb, 1)
        return carry

    lax.fori_loop(0, n_blocks - 1, body, 0, unroll=True)
    value(n_blocks - 1, 0)
    expo(1)
    value(n_blocks - 1, 1)

    pool, conv = _pool_conv(mix_ref, prev_ref, next_ref, tile == 0, tile == pl.num_programs(1) - 1,
                            tile_start, seq_len, pool_w_ref, pool_scale_ref, conv_w_ref)
    y_ref[:, ATTN_WIDTH:ATTN_WIDTH + POOL_WIDTH] = pool.astype(_BF16)
    y_ref[:, ATTN_WIDTH + POOL_WIDTH:] = conv.astype(_BF16)
    y = _dot(y_ref[...], w_out_ref[...])
    o_ref[...] = x_ref[...] + mod_ref[5:6, :] * y


def _mix(x3d, mods, layer, mod_row_of_batch, q, kd, vd, cache, bias, sink, mix_in, pool_w, pool_scale, conv_w, w_out,
         tq, latent):
    n_batch, seq_len, _ = x3d.shape
    n_tiles = seq_len // tq
    halo_blocks = seq_len // HALO
    per_tile = tq // HALO
    tile_map = lambda b, i: (b, i, 0)
    seq_map = lambda b, i: (b, 0, 0)
    per_layer = lambda b, i: (layer, 0, 0)
    n_keys = LOCAL_LEN + cache[0].shape[2] if latent else seq_len
    in_specs = [
        pl.BlockSpec(memory_space=pltpu.SMEM),
        pl.BlockSpec((None, tq, D_MODEL), tile_map),
        pl.BlockSpec((None, None, N_MOD, D_MODEL), lambda b, i: (layer, mod_row_of_batch(b), 0, 0)),
        pl.BlockSpec((None, tq, ATTN_WIDTH), tile_map),
        pl.BlockSpec((None, seq_len, K_SLAB), seq_map),
        pl.BlockSpec((None, seq_len, V_SLAB), seq_map),
    ]
    args = [sink, x3d, mods, q, kd, vd]
    if latent:
        ckd, cvd = cache
        in_specs += [
            pl.BlockSpec((None, None) + ckd.shape[2:], lambda b, i: (layer, b, 0, 0)),
            pl.BlockSpec((None, None) + cvd.shape[2:], lambda b, i: (layer, b, 0, 0)),
            _resident(bias.shape, lambda b, i: (0, 0, 0)),
        ]
        args += [ckd, cvd, bias]
    in_specs += [
        pl.BlockSpec((None, tq, MIX_IN_WIDTH), tile_map),
        pl.BlockSpec((None, HALO, MIX_IN_WIDTH), lambda b, i: (b, jnp.maximum(i * per_tile - 1, 0), 0)),
        pl.BlockSpec((None, HALO, MIX_IN_WIDTH),
                     lambda b, i: (b, jnp.minimum((i + 1) * per_tile, halo_blocks - 1), 0)),
        _resident((None, POOL_WIDTH, POOL_WIDTH), per_layer),
        pl.BlockSpec((None, 1, POOL_WIDTH), per_layer),
        pl.BlockSpec((None, 3, CONV_WIDTH), per_layer),
        _resident((None, D_MODEL, D_MODEL), per_layer),
    ]
    args += [mix_in, mix_in, mix_in, pool_w, pool_scale, conv_w, w_out]
    return pl.pallas_call(
        functools.partial(_mix_kernel, latent=latent, seq_len=seq_len),
        grid=(n_batch, n_tiles),
        in_specs=in_specs,
        out_specs=pl.BlockSpec((None, tq, D_MODEL), tile_map),
        out_shape=jax.ShapeDtypeStruct(x3d.shape, _F32),
        scratch_shapes=[pltpu.VMEM((tq, D_MODEL), _BF16)]
        + [pltpu.VMEM((n_keys, GQA_GROUP * BLOCK), _F32)] * 2 + [pltpu.VMEM((HALO, GQA_GROUP * BLOCK), _F32)] * 2
        + [pltpu.VMEM((n_keys, GQA_GROUP * BLOCK), _BF16)] * 2 + [pltpu.VMEM((HALO, GQA_GROUP * BLOCK), _F32)] * 2,
        name="mixer_latent" if latent else "mixer_context",
        compiler_params=pltpu.CompilerParams(
            dimension_semantics=("arbitrary", "arbitrary"), vmem_limit_bytes=VMEM_LIMIT),
    )(*args)


def _rope_tables(seq_len):
    rows = seq_len // GRID_W
    row = np.repeat(np.arange(rows), GRID_W).astype(np.float32)
    col = np.tile(np.arange(GRID_W), rows).astype(np.float32)
    half = HEAD_DIM // 2
    inv = (np.float32(ROPE_BASE) ** (-np.arange(0, half, 2, dtype=np.float32) / np.float32(half))).astype(np.float32)
    zeros = np.zeros((seq_len, half // 2), np.float32)

    def tables(pos):
        ang = (pos[:, None] * inv[None, :]).astype(np.float32).astype(np.float64)
        cos, sin = np.cos(ang).astype(np.float32), np.sin(ang).astype(np.float32)
        return (np.concatenate([cos, cos], axis=1), np.concatenate([-sin, zeros], axis=1),
                np.concatenate([zeros, sin], axis=1))

    per_head = [np.concatenate([a, b], axis=1) for a, b in zip(tables(row), tables(col))]
    return [jnp.asarray(np.concatenate([t, t], axis=1)) for t in per_head]


def _window_bias():
    r = np.arange(BLOCK)[None, None, :]
    c = np.arange(LOCAL_LEN)[None, :, None]
    v = np.arange(3)[:, None, None]
    return jnp.asarray(np.where(np.abs(c - BLOCK * v - r) <= WINDOW, 0.0, NEG).astype(np.float32))


def _key_slabs(t):
    t = jnp.swapaxes(t, 0, 1)
    return jnp.concatenate([t[..., 0, :], t[..., 0, :], t[..., 1, :], t[..., 1, :]], axis=-1).astype(_BF16)


def _value_slabs(t):
    t = jnp.swapaxes(t, 0, 1)
    ones = jnp.ones(t.shape[:3] + (HEAD_DIM,), t.dtype)
    return jnp.concatenate([t[..., 0, :], ones, t[..., 1, :], ones], axis=-1).astype(_BF16)


def kernel(x_prompt, x_sample, cache_k, cache_v, c, c_ctx, w_ada, b_ada, norm_g, w_ffn_in, w_ffn_out, w_in, w_out,
           q_norm_g, k_norm_g, sink, pool_w, pool_scale, conv_w):
    n_ctx, ctx_len, _ = x_prompt.shape
    n_lat, lat_len, _ = x_sample.shape
    ctx_row = n_lat

    cond = jnp.concatenate([c, c_ctx[None, :], jnp.zeros((COND_ROWS - n_lat - 1, D_MODEL), _F32)], axis=0)
    mods = _ada_rows(cond, w_ada, b_ada).reshape(DEPTH, COND_ROWS, N_MOD, D_MODEL)

    w_ffn_in_b = w_ffn_in.astype(_BF16)
    w_ffn_out_b = w_ffn_out.astype(_BF16)
    w_in_b = w_in.astype(_BF16)
    w_out_b = w_out.astype(_BF16)
    eye = jnp.eye(len(POOL_WINDOWS), dtype=_F32)
    pool_w_b = jnp.einsum('lgcd,gh->lgchd', pool_w, eye).reshape(DEPTH, POOL_WIDTH, POOL_WIDTH).astype(_BF16)
    head_of = np.arange(MXU_COLS) // HEAD_DIM
    seg = jnp.asarray(np.where(head_of[:, None] == head_of[None, :], 1.0 / HEAD_DIM, 0.0), _BF16)
    gq = jnp.tile(q_norm_g, (1, N_HEADS)).reshape(DEPTH, 1, ATTN_WIDTH)
    gk = jnp.tile(k_norm_g, (1, N_KV_HEADS)).reshape(DEPTH, 1, KV_WIDTH)
    pool_scale3 = pool_scale.reshape(DEPTH, 1, POOL_WIDTH)
    rope = _rope_tables(lat_len)
    bias = _window_bias()
    cache = (_key_slabs(cache_k), _value_slabs(cache_v))

    tm = 1024
    tq_lat = 1024
    groups = {
        'ctx': dict(x=x_prompt, latent=False, tq=ctx_len,
                    tile_row=lambda i: ctx_row, batch_row=lambda b: ctx_row),
        'lat': dict(x=x_sample, latent=True, tq=tq_lat,
                    tile_row=lambda i: i // (lat_len // tm), batch_row=lambda b: b),
    }
    new_k, new_v = [], []
    for l in range(DEPTH):
        for name in ('ctx', 'lat'):
            gr = groups[name]
            x3d = gr['x']
            n_batch, seq_len, _ = x3d.shape
            x2d = x3d.reshape(n_batch * seq_len, D_MODEL)
            x2d = _ffn(x2d, mods, l, gr['tile_row'], 0, norm_g, w_ffn_in_b, w_ffn_out_b, tm)
            outs = _proj(x2d, mods, l, gr['tile_row'], norm_g, w_in_b, gq, gk, seg, rope, seq_len, tm, gr['latent'])
            q, kd, vd, mix_in = outs[:4]
            if not gr['latent']:
                new_k.append(outs[4].reshape(n_batch, seq_len, N_KV_HEADS, HEAD_DIM))
                new_v.append(outs[5].reshape(n_batch, seq_len, N_KV_HEADS, HEAD_DIM))
            to3d = lambda t: t.reshape(n_batch, seq_len, t.shape[-1])
            x3d = _mix(x2d.reshape(x3d.shape), mods, l, gr['batch_row'], to3d(q), to3d(kd), to3d(vd),
                       cache if gr['latent'] else None, bias, sink[l], to3d(mix_in), pool_w_b, pool_scale3,
                       conv_w, w_out_b, gr['tq'], gr['latent'])
            x2d = x3d.reshape(n_batch * seq_len, D_MODEL)
            x2d = _ffn(x2d, mods, l, gr['tile_row'], 2, norm_g, w_ffn_in_b, w_ffn_out_b, tm)
            gr['x'] = x2d.reshape(x3d.shape)
    return (groups['ctx']['x'], groups['lat']['x'], jnp.stack(new_k, axis=1), jnp.stack(new_v, axis=1))
```

```python
import functools
import math

import numpy as np
import jax
import jax.numpy as jnp
from jax import lax
from jax.experimental import pallas as pl
from jax.experimental.pallas import tpu as pltpu

D_MODEL = 1024
DEPTH = 2
GRID_W = 64
HEAD_DIM = 64
ATTN_WIDTH = 512
N_HEADS = 8
N_KV_HEADS = 2
GQA_GROUP = 4
KV_WIDTH = 128
POOL_WIDTH = 256
POOL_WINDOWS = (2, 4, 8, 16)
POOL_GROUP = 64
CONV_WIDTH = 256
WINDOW = 128
BLOCK = 128
D_FF = 2816
ROPE_BASE = 10000.0
EPS = 1e-6
NEG = -1e30
N_MOD = 9
IN_WIDTH = 1792
MIX_IN_WIDTH = POOL_WIDTH + 2 * CONV_WIDTH
LOG2E = math.log2(math.e)

LANES = 128
HALO = 8
MXU_COLS = 256
FF_CHUNK = MXU_COLS
N_FF_CHUNKS = D_FF // FF_CHUNK
COND_ROWS = 16
ADA_CHUNK = 1152
K_SLAB = 2 * KV_WIDTH
V_SLAB = 2 * LANES
LOCAL_LEN = 3 * BLOCK
VMEM_LIMIT = 60 * 1024 * 1024

_BF16 = jnp.bfloat16
_F32 = jnp.float32


def _dot(a, b):
    return jnp.dot(a, b, preferred_element_type=_F32)


def _dot_nt(a, b):
    return lax.dot_general(a, b, (((1,), (1,)), ((), ())), preferred_element_type=_F32)


def _dot_tn(a, b):
    return lax.dot_general(a, b, (((0,), (0,)), ((), ())), preferred_element_type=_F32)


def _silu(x):
    return x * (1.0 / (1.0 + jnp.exp(-x)))


def _modulated_norm(x, g, shift, scale):
    y = x * lax.rsqrt(jnp.mean(x * x, axis=-1, keepdims=True) + EPS)
    return (y * g) * (1.0 + scale) + shift


def _resident(shape, index_map):
    return pl.BlockSpec(shape, index_map, pipeline_mode=pl.Buffered(1))


def _ada_kernel(cond_ref, w_ref, b_ref, o_ref):
    s = _silu(cond_ref[...]).astype(_BF16)
    o_ref[...] = _dot(s, w_ref[...].astype(_BF16)) + b_ref[...]


def _ada_rows(cond, w_ada, b_ada):
    n_out = N_MOD * D_MODEL
    return pl.pallas_call(
        _ada_kernel,
        grid=(DEPTH, n_out // ADA_CHUNK),
        in_specs=[
            pl.BlockSpec((COND_ROWS, D_MODEL), lambda l, j: (0, 0)),
            pl.BlockSpec((None, D_MODEL, ADA_CHUNK), lambda l, j: (l, 0, j)),
            pl.BlockSpec((None, 1, ADA_CHUNK), lambda l, j: (l, 0, j)),
        ],
        out_specs=pl.BlockSpec((None, COND_ROWS, ADA_CHUNK), lambda l, j: (l, 0, j)),
        out_shape=jax.ShapeDtypeStruct((DEPTH, COND_ROWS, n_out), _F32),
        name="ada_rows",
        compiler_params=pltpu.CompilerParams(
            dimension_semantics=("arbitrary", "arbitrary"), vmem_limit_bytes=VMEM_LIMIT),
    )(cond, w_ada, b_ada.reshape(DEPTH, 1, n_out))


def _ffn_tile(x_ref, mod_ref, g_ref, w_in_ref, w_out_ref, o_ref, k):
    x = x_ref[...]
    shift = mod_ref[3 * k:3 * k + 1, :]
    scale = mod_ref[3 * k + 1:3 * k + 2, :]
    gate = mod_ref[3 * k + 2:3 * k + 3, :]
    h = _modulated_norm(x, g_ref[k:k + 1, :], shift, scale).astype(_BF16)
    acc = jnp.zeros(x.shape, _F32)
    for c in range(N_FF_CHUNKS):
        gt = _dot(h, w_in_ref[c])
        up = _dot(h, w_in_ref[N_FF_CHUNKS + c])
        a = (_silu(gt) * up).astype(_BF16)
        acc = acc + _dot(a, w_out_ref[c])
    o_ref[...] = x + (0.5 * gate) * acc


def _ffn_tile_staged(x_ref, mod_ref, g_ref, w_in_ref, w_out_ref, o_ref, h_ref, k, zero, before_chunk):
    shift = mod_ref[3 * k:3 * k + 1, :]
    scale = mod_ref[3 * k + 1:3 * k + 2, :]
    gate = mod_ref[3 * k + 2:3 * k + 3, :]
    h_ref[...] = _modulated_norm(x_ref[...], g_ref[k:k + 1, :], shift, scale).astype(_BF16)
    for c in range(N_FF_CHUNKS):
        before_chunk(c)
        h = h_ref[...]
        gt = _dot(h, w_in_ref[c + zero])
        up = _dot(h, w_in_ref[N_FF_CHUNKS + c + zero])
        down = _dot((_silu(gt) * up).astype(_BF16), w_out_ref[c + zero])
        o_ref[...] = down if c == 0 else o_ref[...] + down
    o_ref[...] = x_ref[...] + (0.5 * gate) * o_ref[...]


def _weight_stager(w_in_hbm, w_out_hbm, layer, which, w_in_ref, w_out_ref, stage_ref, sems):
    half = D_MODEL // 2
    gate_win = (slice(0, D_MODEL), slice(0, FF_CHUNK))
    up_win = (slice(0, D_MODEL), slice(FF_CHUNK, 2 * FF_CHUNK))
    down_wins = [(slice(j * FF_CHUNK, (j + 1) * FF_CHUNK), slice(half, 2 * half)) for j in range(2)]
    wins = [gate_win, up_win] + down_wins

    def sources(c):
        lo = FF_CHUNK * c
        return [w_in_hbm.at[layer, which, :, pl.ds(lo, FF_CHUNK)],
                w_in_hbm.at[layer, which, :, pl.ds(D_FF + lo, FF_CHUNK)],
                w_out_hbm.at[layer, which, pl.ds(lo, FF_CHUNK), pl.ds(0, half)],
                w_out_hbm.at[layer, which, pl.ds(lo, FF_CHUNK), pl.ds(half, half)]]

    def copies(c):
        return [pltpu.make_async_copy(src, stage_ref.at[pl.ds(rows.start, rows.stop - rows.start),
                                                         pl.ds(cols.start, cols.stop - cols.start)], sems.at[j])
                for j, (src, (rows, cols)) in enumerate(zip(sources(c), wins))]

    def stage(c):
        if c == 0:
            for cp in copies(0):
                cp.start()
        for cp in copies(c):
            cp.wait()
        w_in_ref[c] = stage_ref[gate_win].astype(_BF16)
        w_in_ref[N_FF_CHUNKS + c] = stage_ref[up_win].astype(_BF16)
        for j, win in enumerate(down_wins):
            w_out_ref[c, :, j * half:(j + 1) * half] = stage_ref[win].astype(_BF16)
        if c + 1 < N_FF_CHUNKS:
            for cp in copies(c + 1):
                cp.start()

    return stage


def _ffn_kernel(zero_ref, xc_ref, xl_ref, mod_ref, g_ref, w_in_hbm, w_out_hbm, oc_ref, ol_ref,
                w_in_ref, w_out_ref, h_ref, sems, *, k, n_ctx_tiles, layer):
    which = k // 2
    step = pl.program_id(0)
    stage = _weight_stager(w_in_hbm, w_out_hbm, layer, which, w_in_ref, w_out_ref, ol_ref, sems)

    @pl.when(step < n_ctx_tiles)
    def _():
        _ffn_tile_staged(xc_ref, mod_ref, g_ref, w_in_ref, w_out_ref, oc_ref, h_ref, k, zero_ref[0],
                         before_chunk=lambda c: pl.when(step == 0)(lambda: stage(c)))

    @pl.when(step >= n_ctx_tiles)
    def _():
        _ffn_tile(xl_ref, mod_ref, g_ref, w_in_ref, w_out_ref, ol_ref, k)


def _ffn(xc2d, xl2d, mods, layer, ctx_row, lat_len, k, norm_g, w_in, w_out, tm):
    nc, nl = xc2d.shape[0] // tm, xl2d.shape[0] // tm
    tiles_per_seq = lat_len // tm
    assert tm == D_MODEL and nc >= 1
    ctx_tile = lambda i: (jnp.minimum(i, nc - 1), 0)
    lat_tile = lambda i: (jnp.maximum(i - nc, 0), 0)
    mod_row = lambda i: jnp.where(i < nc, ctx_row, jnp.maximum(i - nc, 0) // tiles_per_seq)
    return pl.pallas_call(
        functools.partial(_ffn_kernel, k=k, n_ctx_tiles=nc, layer=layer),
        grid=(nc + nl,),
        in_specs=[
            pl.BlockSpec(memory_space=pltpu.SMEM),
            pl.BlockSpec((tm, D_MODEL), ctx_tile),
            pl.BlockSpec((tm, D_MODEL), lat_tile),
            pl.BlockSpec((None, None, N_MOD, D_MODEL), lambda i: (layer, mod_row(i), 0, 0)),
            pl.BlockSpec((None, 3, D_MODEL), lambda i: (layer, 0, 0)),
            pl.BlockSpec(memory_space=pl.ANY),
            pl.BlockSpec(memory_space=pl.ANY),
        ],
        out_specs=[pl.BlockSpec((tm, D_MODEL), ctx_tile), pl.BlockSpec((tm, D_MODEL), lat_tile)],
        out_shape=[jax.ShapeDtypeStruct(xc2d.shape, _F32), jax.ShapeDtypeStruct(xl2d.shape, _F32)],
        scratch_shapes=[
            pltpu.VMEM((2 * N_FF_CHUNKS, D_MODEL, FF_CHUNK), _BF16),
            pltpu.VMEM((N_FF_CHUNKS, FF_CHUNK, D_MODEL), _BF16),
            pltpu.VMEM((tm, D_MODEL), _BF16),
            pltpu.SemaphoreType.DMA((4,)),
        ],
        name="ffn_half_step",
        compiler_params=pltpu.CompilerParams(
            dimension_semantics=("arbitrary",), vmem_limit_bytes=VMEM_LIMIT),
    )(jnp.zeros((1,), jnp.int32), xc2d, xl2d, mods, norm_g, w_in, w_out)


def _rotate(v, cos, sin_lo, sin_hi):
    n = v.shape[-1]
    quarter = HEAD_DIM // 4
    return v * cos + pltpu.roll(v, n - quarter, 1) * sin_lo + pltpu.roll(v, quarter, 1) * sin_hi


def _proj_rows(rs, refs, latent):
    if latent:
        (x_ref, mod_ref, g_ref, w_ref, gq_ref, gk_ref, seg_ref, cos_ref, slo_ref, shi_ref,
         q_ref, kd_ref, vd_ref, mix_ref) = refs
    else:
        (x_ref, mod_ref, g_ref, w_ref, gq_ref, gk_ref, seg_ref,
         q_ref, kd_ref, vd_ref, mix_ref, k32_ref, v32_ref) = refs
    x = x_ref[rs, :]
    h = _modulated_norm(x, g_ref[1:2, :], mod_ref[3:4, :], mod_ref[4:5, :]).astype(_BF16)
    p = _dot(h, w_ref[...])
    q = p[:, :ATTN_WIDTH]
    kk = p[:, ATTN_WIDTH:ATTN_WIDTH + KV_WIDTH]
    v = p[:, ATTN_WIDTH + KV_WIDTH:ATTN_WIDTH + 2 * KV_WIDTH]
    base = ATTN_WIDTH + 2 * KV_WIDTH
    u_pool = p[:, base:base + POOL_WIDTH]
    u_conv = p[:, base + POOL_WIDTH:base + POOL_WIDTH + CONV_WIDTH]
    bgate = p[:, base + POOL_WIDTH + CONV_WIDTH:base + POOL_WIDTH + 2 * CONV_WIDTH]
    cgate = p[:, base + POOL_WIDTH + 2 * CONV_WIDTH:]

    def head_norm(t, seg):
        return t * lax.rsqrt(_dot((t * t).astype(_BF16), seg) + EPS)

    qn = jnp.concatenate(
        [head_norm(q[:, j * MXU_COLS:(j + 1) * MXU_COLS], seg_ref[...]) for j in range(ATTN_WIDTH // MXU_COLS)],
        axis=1) * gq_ref[...]
    kn = head_norm(kk, seg_ref[:KV_WIDTH, :KV_WIDTH]) * gk_ref[...]
    if not latent:
        k32_ref[rs, :] = kn
        v32_ref[rs, :] = v
    else:
        cos, slo, shi = cos_ref[rs, :], slo_ref[rs, :], shi_ref[rs, :]
        tile4 = lambda t: jnp.concatenate([t] * (ATTN_WIDTH // KV_WIDTH), axis=1)
        qn = _rotate(qn, tile4(cos), tile4(slo), tile4(shi))
        kn = _rotate(kn, cos, slo, shi)
    q_ref[rs, :] = (qn * (HEAD_DIM ** -0.5 * LOG2E)).astype(_BF16)

    low = lax.broadcasted_iota(jnp.int32, kn.shape, 1) < HEAD_DIM
    ksw = pltpu.roll(kn, HEAD_DIM, 1)
    kd_ref[rs, :KV_WIDTH] = jnp.where(low, kn, ksw).astype(_BF16)
    kd_ref[rs, KV_WIDTH:] = jnp.where(low, ksw, kn).astype(_BF16)
    vd_ref[rs, :LANES] = jnp.where(low, v, 1.0).astype(_BF16)
    vd_ref[rs, LANES:] = jnp.where(low, pltpu.roll(v, HEAD_DIM, 1), 1.0).astype(_BF16)
    mix_ref[rs, :POOL_WIDTH] = u_pool
    mix_ref[rs, POOL_WIDTH:POOL_WIDTH + CONV_WIDTH] = cgate * u_conv
    mix_ref[rs, POOL_WIDTH + CONV_WIDTH:] = bgate


def _proj_kernel(*refs, latent, sub_rows):
    for r0 in range(0, refs[0].shape[0], sub_rows):
        _proj_rows(slice(r0, r0 + sub_rows), refs, latent)


def _proj(x2d, mods, layer, mod_row_of_tile, norm_g, w, gq, gk, seg, rope, seq_len, tm, latent):
    n_tok = x2d.shape[0]
    const = lambda i: (0, 0)
    row = lambda i: (i, 0)
    per_layer = lambda i: (layer, 0, 0)
    in_specs = [
        pl.BlockSpec((tm, D_MODEL), row),
        pl.BlockSpec((None, None, N_MOD, D_MODEL), lambda i: (layer, mod_row_of_tile(i), 0, 0)),
        pl.BlockSpec((None, 3, D_MODEL), per_layer),
        _resident((None, D_MODEL, IN_WIDTH), per_layer),
        pl.BlockSpec((None, 1, ATTN_WIDTH), per_layer),
        pl.BlockSpec((None, 1, KV_WIDTH), per_layer),
        _resident((MXU_COLS, MXU_COLS), const),
    ]
    args = [x2d, mods, norm_g, w, gq, gk, seg]
    out_specs = [
        pl.BlockSpec((tm, ATTN_WIDTH), row),
        pl.BlockSpec((tm, K_SLAB), row),
        pl.BlockSpec((tm, V_SLAB), row),
        pl.BlockSpec((tm, MIX_IN_WIDTH), row),
    ]
    out_shape = [
        jax.ShapeDtypeStruct((n_tok, ATTN_WIDTH), _BF16),
        jax.ShapeDtypeStruct((n_tok, K_SLAB), _BF16),
        jax.ShapeDtypeStruct((n_tok, V_SLAB), _BF16),
        jax.ShapeDtypeStruct((n_tok, MIX_IN_WIDTH), _F32),
    ]
    if latent:
        tiles_per_seq = seq_len // tm
        pos = lambda i: (i % tiles_per_seq, 0)
        in_specs += [pl.BlockSpec((tm, KV_WIDTH), pos)] * 3
        args += list(rope)
    else:
        out_specs += [pl.BlockSpec((tm, KV_WIDTH), row)] * 2
        out_shape += [jax.ShapeDtypeStruct((n_tok, KV_WIDTH), _F32)] * 2
    return pl.pallas_call(
        functools.partial(_proj_kernel, latent=latent, sub_rows=tm // 4),
        grid=(n_tok // tm,),
        in_specs=in_specs,
        out_specs=out_specs,
        out_shape=out_shape,
        name="mixer_proj_latent" if latent else "mixer_proj_context",
        compiler_params=pltpu.CompilerParams(
            dimension_semantics=("arbitrary",), vmem_limit_bytes=VMEM_LIMIT),
    )(*args)


def _block_row(jb):
    return jb * BLOCK if isinstance(jb, int) else pl.multiple_of(jb * BLOCK, BLOCK)


def _sink_row(io, kh):
    return jnp.concatenate(
        [jnp.full((1, BLOCK), io['sink'][kh * GQA_GROUP + g] * LOG2E, _F32) for g in range(GQA_GROUP)], axis=1)


def _score_stage(jb, kh, io, s_ref, m_ref):
    latent, seq_len, tile_start, low = io['latent'], io['seq_len'], io['tile_start'], io['low']
    row0 = _block_row(jb)
    qb = io['q'][pl.ds(row0, BLOCK), kh * GQA_GROUP * HEAD_DIM:(kh + 1) * GQA_GROUP * HEAD_DIM]
    parts = []
    for g in range(GQA_GROUP):
        pair = qb[:, (g // 2) * LANES:(g // 2 + 1) * LANES]
        keep = low if g % 2 == 0 else jnp.logical_not(low)
        parts.append(jnp.where(keep, pair, jnp.zeros_like(pair)))
    qs = jnp.concatenate(parts, axis=0)
    ksl = slice(kh * KV_WIDTH, (kh + 1) * KV_WIDTH)
    if latent:
        q0 = tile_start + row0
        kstart = pl.multiple_of(jnp.clip(q0 - BLOCK, 0, seq_len - LOCAL_LEN), BLOCK)
        variant = jnp.where(q0 == 0, 0, jnp.where(q0 == seq_len - BLOCK, 2, 1))
        bias = io['bias'][variant]
        s_loc = _dot_nt(io['kd'][pl.ds(kstart, LOCAL_LEN), ksl], qs) + jnp.concatenate([bias] * GQA_GROUP, axis=1)
        pieces = [s_loc, _dot_nt(io['ckd'][:, ksl], qs)]
    else:
        pieces = [_dot_nt(io['kd'][:, ksl], qs)]
    m = _sink_row(io, kh)
    off = 0
    for piece in pieces:
        m = jnp.maximum(m, jnp.max(piece, axis=0, keepdims=True))
        s_ref[off:off + piece.shape[0], :] = piece
        off += piece.shape[0]
    m_ref[...] = jnp.broadcast_to(m, m_ref.shape)


def _exp_stage(kh, io, s_ref, m_ref, p_ref, e_ref):
    m = m_ref[0:1, :]
    for r0 in range(0, s_ref.shape[0], BLOCK):
        p_ref[r0:r0 + BLOCK, :] = jnp.exp2(s_ref[r0:r0 + BLOCK, :] - m).astype(_BF16)
    e_ref[...] = jnp.broadcast_to(jnp.exp2(_sink_row(io, kh) - m), e_ref.shape)


def _value_stage(jb, kh, io, p_ref, e_ref):
    latent, seq_len, tile_start = io['latent'], io['seq_len'], io['tile_start']
    row0 = _block_row(jb)
    vsl = slice(kh * LANES, (kh + 1) * LANES)
    if latent:
        kstart = pl.multiple_of(jnp.clip(tile_start + row0 - BLOCK, 0, seq_len - LOCAL_LEN), BLOCK)
        acc = (_dot_tn(io['vd'][pl.ds(kstart, LOCAL_LEN), vsl], p_ref[:LOCAL_LEN, :])
               + _dot_tn(io['cvd'][:, vsl], p_ref[LOCAL_LEN:, :]))
    else:
        acc = _dot_tn(io['vd'][:, vsl], p_ref[...])
    den = acc[HEAD_DIM:HEAD_DIM + 1, :] + e_ref[0:1, :]
    o = acc[:HEAD_DIM, :] / den
    for j in range(GQA_GROUP // 2):
        pair = jnp.concatenate([o[:, 2 * j * BLOCK:(2 * j + 1) * BLOCK], o[:, (2 * j + 1) * BLOCK:(2 * j + 2) * BLOCK]],
                               axis=0)
        col = (kh * (GQA_GROUP // 2) + j) * LANES
        io['y'][pl.ds(row0, BLOCK), col:col + LANES] = pair.T.astype(_BF16)


def _pool_conv(mix_ref, prev_ref, next_ref, first_tile, last_tile, tile_start, seq_len,
               pool_w_ref, pool_scale_ref, conv_w_ref):
    tq = mix_ref.shape[0]
    prev = jnp.where(first_tile, 0.0, prev_ref[...])
    nxt = jnp.where(last_tile, 0.0, next_ref[...])
    cur = mix_ref[...]
    pad = jnp.concatenate([prev, cur, nxt], axis=0)

    up = pad[:, :POOL_WIDTH]
    w2 = up[1:] + up[:-1]
    w4 = w2[:-2] + w2[2:]
    w8 = w4[:-4] + w4[4:]
    w16 = w8[:-8] + w8[8:]
    lane = lax.broadcasted_iota(jnp.int32, (tq, POOL_WIDTH), 1)
    group = lane // POOL_GROUP
    win = jnp.where(group == 0, w2[7:7 + tq],
                    jnp.where(group == 1, w4[6:6 + tq],
                              jnp.where(group == 2, w8[4:4 + tq], w16[:tq])))
    tpos = tile_start + lax.broadcasted_iota(jnp.int32, (tq, POOL_WIDTH), 0)
    half = jnp.left_shift(1, group)
    cnt = (jnp.minimum(tpos + half, seq_len) - jnp.maximum(tpos - half, 0)).astype(_F32)
    u = cur[:, :POOL_WIDTH]
    pooled = (win / cnt - u).astype(_BF16)
    pool = _dot(pooled, pool_w_ref[...]) * pool_scale_ref[...]

    uc = pad[:, POOL_WIDTH:POOL_WIDTH + CONV_WIDTH]
    conv = (uc[HALO - 1:HALO - 1 + tq] * conv_w_ref[0:1, :] + uc[HALO:HALO + tq] * conv_w_ref[1:2, :]
            + uc[HALO + 1:HALO + 1 + tq] * conv_w_ref[2:3, :])
    conv = cur[:, POOL_WIDTH + CONV_WIDTH:] * conv
    return pool, conv


def _mix_kernel(*refs, latent, seq_len):
    if latent:
        (sink_ref, x_ref, mod_ref, q_ref, kd_ref, vd_ref, ckd_ref, cvd_ref, bias_ref, mix_ref, prev_ref, next_ref,
         pool_w_ref, pool_scale_ref, conv_w_ref, w_out_ref, o_ref, y_ref, s0_ref, s1_ref, m0_ref, m1_ref, p0_ref, p1_ref, e0_ref, e1_ref) = refs
    else:
        (sink_ref, x_ref, mod_ref, q_ref, kd_ref, vd_ref, mix_ref, prev_ref, next_ref,
         pool_w_ref, pool_scale_ref, conv_w_ref, w_out_ref, o_ref, y_ref, s0_ref, s1_ref, m0_ref, m1_ref, p0_ref, p1_ref, e0_ref, e1_ref) = refs
        ckd_ref = cvd_ref = bias_ref = None
    tq = x_ref.shape[0]
    n_blocks = tq // BLOCK
    tile = pl.program_id(1)
    tile_start = tile * tq
    io = dict(latent=latent, seq_len=seq_len, tile_start=tile_start, sink=sink_ref, q=q_ref, kd=kd_ref, vd=vd_ref,
              ckd=ckd_ref, cvd=cvd_ref, bias=bias_ref, y=y_ref,
              low=lax.broadcasted_iota(jnp.int32, (BLOCK, LANES), 1) < HEAD_DIM)

    bufs = ((s0_ref, m0_ref, p0_ref, e0_ref), (s1_ref, m1_ref, p1_ref, e1_ref))
    score = lambda jb, kh: _score_stage(jb, kh, io, bufs[kh][0], bufs[kh][1])
    expo = lambda kh: _exp_stage(kh, io, *bufs[kh])
    value = lambda jb, kh: _value_stage(jb, kh, io, bufs[kh][2], bufs[kh][3])

    score(0, 0)
    expo(0)
    score(0, 1)

    def body(jb, carry):
        score(jb + 1, 0)
        expo(1)
        value(jb, 0)
        score(jb + 1, 1)
        expo(0)
        value(jb, 1)
        return carry

    lax.fori_loop(0, n_blocks - 1, body, 0, unroll=True)
    value(n_blocks - 1, 0)
    expo(1)
    value(n_blocks - 1, 1)

    pool, conv = _pool_conv(mix_ref, prev_ref, next_ref, tile == 0, tile == pl.num_programs(1) - 1,
                            tile_start, seq_len, pool_w_ref, pool_scale_ref, conv_w_ref)
    y_ref[:, ATTN_WIDTH:ATTN_WIDTH + POOL_WIDTH] = pool.astype(_BF16)
    y_ref[:, ATTN_WIDTH + POOL_WIDTH:] = conv.astype(_BF16)
    y = _dot(y_ref[...], w_out_ref[...])
    o_ref[...] = x_ref[...] + mod_ref[5:6, :] * y


def _mix(x3d, mods, layer, mod_row_of_batch, q, kd, vd, cache, bias, sink, mix_in, pool_w, pool_scale, conv_w, w_out,
         tq, latent):
    n_batch, seq_len, _ = x3d.shape
    n_tiles = seq_len // tq
    halo_blocks = seq_len // HALO
    per_tile = tq // HALO
    tile_map = lambda b, i: (b, i, 0)
    seq_map = lambda b, i: (b, 0, 0)
    per_layer = lambda b, i: (layer, 0, 0)
    n_keys = LOCAL_LEN + cache[0].shape[2] if latent else seq_len
    in_specs = [
        pl.BlockSpec(memory_space=pltpu.SMEM),
        pl.BlockSpec((None, tq, D_MODEL), tile_map),
        pl.BlockSpec((None, None, N_MOD, D_MODEL), lambda b, i: (layer, mod_row_of_batch(b), 0, 0)),
        pl.BlockSpec((None, tq, ATTN_WIDTH), tile_map),
        pl.BlockSpec((None, seq_len, K_SLAB), seq_map),
        pl.BlockSpec((None, seq_len, V_SLAB), seq_map),
    ]
    args = [sink, x3d, mods, q, kd, vd]
    if latent:
        ckd, cvd = cache
        in_specs += [
            pl.BlockSpec((None, None) + ckd.shape[2:], lambda b, i: (layer, b, 0, 0)),
            pl.BlockSpec((None, None) + cvd.shape[2:], lambda b, i: (layer, b, 0, 0)),
            _resident(bias.shape, lambda b, i: (0, 0, 0)),
        ]
        args += [ckd, cvd, bias]
    in_specs += [
        pl.BlockSpec((None, tq, MIX_IN_WIDTH), tile_map),
        pl.BlockSpec((None, HALO, MIX_IN_WIDTH), lambda b, i: (b, jnp.maximum(i * per_tile - 1, 0), 0)),
        pl.BlockSpec((None, HALO, MIX_IN_WIDTH),
                     lambda b, i: (b, jnp.minimum((i + 1) * per_tile, halo_blocks - 1), 0)),
        _resident((None, POOL_WIDTH, POOL_WIDTH), per_layer),
        pl.BlockSpec((None, 1, POOL_WIDTH), per_layer),
        pl.BlockSpec((None, 3, CONV_WIDTH), per_layer),
        _resident((None, D_MODEL, D_MODEL), per_layer),
    ]
    args += [mix_in, mix_in, mix_in, pool_w, pool_scale, conv_w, w_out]
    return pl.pallas_call(
        functools.partial(_mix_kernel, latent=latent, seq_len=seq_len),
        grid=(n_batch, n_tiles),
        in_specs=in_specs,
        out_specs=pl.BlockSpec((None, tq, D_MODEL), tile_map),
        out_shape=jax.ShapeDtypeStruct(x3d.shape, _F32),
        scratch_shapes=[pltpu.VMEM((tq, D_MODEL), _BF16)]
        + [pltpu.VMEM((n_keys, GQA_GROUP * BLOCK), _F32)] * 2 + [pltpu.VMEM((HALO, GQA_GROUP * BLOCK), _F32)] * 2
        + [pltpu.VMEM((n_keys, GQA_GROUP * BLOCK), _BF16)] * 2 + [pltpu.VMEM((HALO, GQA_GROUP * BLOCK), _F32)] * 2,
        name="mixer_latent" if latent else "mixer_context",
        compiler_params=pltpu.CompilerParams(
            dimension_semantics=("arbitrary", "arbitrary"), vmem_limit_bytes=VMEM_LIMIT),
    )(*args)


def _rope_tables(seq_len):
    rows = seq_len // GRID_W
    row = np.repeat(np.arange(rows), GRID_W).astype(np.float32)
    col = np.tile(np.arange(GRID_W), rows).astype(np.float32)
    half = HEAD_DIM // 2
    inv = (np.float32(ROPE_BASE) ** (-np.arange(0, half, 2, dtype=np.float32) / np.float32(half))).astype(np.float32)
    zeros = np.zeros((seq_len, half // 2), np.float32)

    def tables(pos):
        ang = (pos[:, None] * inv[None, :]).astype(np.float32).astype(np.float64)
        cos, sin = np.cos(ang).astype(np.float32), np.sin(ang).astype(np.float32)
        return (np.concatenate([cos, cos], axis=1), np.concatenate([-sin, zeros], axis=1),
                np.concatenate([zeros, sin], axis=1))

    per_head = [np.concatenate([a, b], axis=1) for a, b in zip(tables(row), tables(col))]
    return [jnp.asarray(np.concatenate([t, t], axis=1)) for t in per_head]


def _window_bias():
    r = np.arange(BLOCK)[None, None, :]
    c = np.arange(LOCAL_LEN)[None, :, None]
    v = np.arange(3)[:, None, None]
    return jnp.asarray(np.where(np.abs(c - BLOCK * v - r) <= WINDOW, 0.0, NEG).astype(np.float32))


def _key_slabs(t):
    t = jnp.swapaxes(t, 0, 1)
    return jnp.concatenate([t[..., 0, :], t[..., 0, :], t[..., 1, :], t[..., 1, :]], axis=-1).astype(_BF16)


def _value_slabs(t):
    t = jnp.swapaxes(t, 0, 1)
    ones = jnp.ones(t.shape[:3] + (HEAD_DIM,), t.dtype)
    return jnp.concatenate([t[..., 0, :], ones, t[..., 1, :], ones], axis=-1).astype(_BF16)


def kernel(x_prompt, x_sample, cache_k, cache_v, c, c_ctx, w_ada, b_ada, norm_g, w_ffn_in, w_ffn_out, w_in, w_out,
           q_norm_g, k_norm_g, sink, pool_w, pool_scale, conv_w):
    n_ctx, ctx_len, _ = x_prompt.shape
    n_lat, lat_len, _ = x_sample.shape
    ctx_row = n_lat

    cond = jnp.concatenate([c, c_ctx[None, :], jnp.zeros((COND_ROWS - n_lat - 1, D_MODEL), _F32)], axis=0)
    mods = _ada_rows(cond, w_ada, b_ada).reshape(DEPTH, COND_ROWS, N_MOD, D_MODEL)

    w_in_b = w_in.astype(_BF16)
    w_out_b = w_out.astype(_BF16)
    eye = jnp.eye(len(POOL_WINDOWS), dtype=_F32)
    pool_w_b = jnp.einsum('lgcd,gh->lgchd', pool_w, eye).reshape(DEPTH, POOL_WIDTH, POOL_WIDTH).astype(_BF16)
    head_of = np.arange(MXU_COLS) // HEAD_DIM
    seg = jnp.asarray(np.where(head_of[:, None] == head_of[None, :], 1.0 / HEAD_DIM, 0.0), _BF16)
    gq = jnp.tile(q_norm_g, (1, N_HEADS)).reshape(DEPTH, 1, ATTN_WIDTH)
    gk = jnp.tile(k_norm_g, (1, N_KV_HEADS)).reshape(DEPTH, 1, KV_WIDTH)
    pool_scale3 = pool_scale.reshape(DEPTH, 1, POOL_WIDTH)
    rope = _rope_tables(lat_len)
    bias = _window_bias()
    cache = (_key_slabs(cache_k), _value_slabs(cache_v))

    tm = 1024
    tq_lat = 1024
    ctx = dict(latent=False, tq=ctx_len, n_batch=n_ctx, seq_len=ctx_len,
               tile_row=lambda i: ctx_row, batch_row=lambda b: ctx_row)
    lat = dict(latent=True, tq=tq_lat, n_batch=n_lat, seq_len=lat_len,
               tile_row=lambda i: i // (lat_len // tm), batch_row=lambda b: b)
    xs = {'ctx': x_prompt.reshape(n_ctx * ctx_len, D_MODEL), 'lat': x_sample.reshape(n_lat * lat_len, D_MODEL)}
    new_k, new_v = [], []
    for l in range(DEPTH):
        xs['ctx'], xs['lat'] = _ffn(xs['ctx'], xs['lat'], mods, l, ctx_row, lat_len, 0, norm_g,
                                    w_ffn_in, w_ffn_out, tm)
        for name, gr in (('ctx', ctx), ('lat', lat)):
            n_batch, seq_len = gr['n_batch'], gr['seq_len']
            x2d = xs[name]
            outs = _proj(x2d, mods, l, gr['tile_row'], norm_g, w_in_b, gq, gk, seg, rope, seq_len, tm, gr['latent'])
            q, kd, vd, mix_in = outs[:4]
            if not gr['latent']:
                new_k.append(outs[4].reshape(n_batch, seq_len, N_KV_HEADS, HEAD_DIM))
                new_v.append(outs[5].reshape(n_batch, seq_len, N_KV_HEADS, HEAD_DIM))
            to3d = lambda t: t.reshape(n_batch, seq_len, t.shape[-1])
            x3d = _mix(to3d(x2d), mods, l, gr['batch_row'], to3d(q), to3d(kd), to3d(vd),
                       cache if gr['latent'] else None, bias, sink[l], to3d(mix_in), pool_w_b, pool_scale3,
                       conv_w, w_out_b, gr['tq'], gr['latent'])
            xs[name] = x3d.reshape(n_batch * seq_len, D_MODEL)
        xs['ctx'], xs['lat'] = _ffn(xs['ctx'], xs['lat'], mods, l, ctx_row, lat_len, 2, norm_g,
                                    w_ffn_in, w_ffn_out, tm)
    return (xs['ctx'].reshape(x_prompt.shape), xs['lat'].reshape(x_sample.shape),
            jnp.stack(new_k, axis=1), jnp.stack(new_v, axis=1))
```

```python
import functools
import math

import numpy as np
import jax
import jax.numpy as jnp
from jax import lax
from jax.experimental import pallas as pl
from jax.experimental.pallas import tpu as pltpu

D_MODEL = 1024
DEPTH = 2
GRID_W = 64
HEAD_DIM = 64
ATTN_WIDTH = 512
N_HEADS = 8
N_KV_HEADS = 2
GQA_GROUP = 4
KV_WIDTH = 128
POOL_WIDTH = 256
POOL_WINDOWS = (2, 4, 8, 16)
POOL_GROUP = 64
CONV_WIDTH = 256
WINDOW = 128
BLOCK = 128
D_FF = 2816
ROPE_BASE = 10000.0
EPS = 1e-6
NEG = -1e30
N_MOD = 9
IN_WIDTH = 1792
MIX_IN_WIDTH = POOL_WIDTH + 2 * CONV_WIDTH
LOG2E = math.log2(math.e)

LANES = 128
HALO = 8
MXU_COLS = 256
FF_CHUNK = MXU_COLS
N_FF_CHUNKS = D_FF // FF_CHUNK
COND_ROWS = 16
ADA_CHUNK = 2304
PROJ_SUB_ROWS = 256
K_SLAB = 2 * KV_WIDTH
V_SLAB = 2 * LANES
LOCAL_LEN = 3 * BLOCK
VMEM_LIMIT = 56 * 1024 * 1024

_BF16 = jnp.bfloat16
_F32 = jnp.float32


def _dot(a, b):
    return jnp.dot(a, b, preferred_element_type=_F32)


def _dot_nt(a, b):
    return lax.dot_general(a, b, (((1,), (1,)), ((), ())), preferred_element_type=_F32)


def _dot_tn(a, b):
    return lax.dot_general(a, b, (((0,), (0,)), ((), ())), preferred_element_type=_F32)


def _silu(x):
    return x * (1.0 / (1.0 + jnp.exp(-x)))


def _modulated_norm(x, g, shift, scale):
    y = x * lax.rsqrt(jnp.mean(x * x, axis=-1, keepdims=True) + EPS)
    return (y * g) * (1.0 + scale) + shift


def _resident(shape, index_map):
    return pl.BlockSpec(shape, index_map, pipeline_mode=pl.Buffered(1))


def _ada_kernel(cond_ref, w_ref, b_ref, o_ref):
    s = _silu(cond_ref[...]).astype(_BF16)
    o_ref[...] = _dot(s, w_ref[...].astype(_BF16)) + b_ref[...]


def _ada_rows(cond, w_ada, b_ada):
    n_out = N_MOD * D_MODEL
    return pl.pallas_call(
        _ada_kernel,
        grid=(DEPTH, n_out // ADA_CHUNK),
        in_specs=[
            pl.BlockSpec((COND_ROWS, D_MODEL), lambda l, j: (0, 0)),
            pl.BlockSpec((None, D_MODEL, ADA_CHUNK), lambda l, j: (l, 0, j)),
            pl.BlockSpec((None, 1, ADA_CHUNK), lambda l, j: (l, 0, j)),
        ],
        out_specs=pl.BlockSpec((None, COND_ROWS, ADA_CHUNK), lambda l, j: (l, 0, j)),
        out_shape=jax.ShapeDtypeStruct((DEPTH, COND_ROWS, n_out), _F32),
        name="ada_rows",
        compiler_params=pltpu.CompilerParams(
            dimension_semantics=("arbitrary", "arbitrary"), vmem_limit_bytes=VMEM_LIMIT),
    )(cond, w_ada, b_ada.reshape(DEPTH, 1, n_out))


def _ffn_tile(x_ref, mod_ref, g_ref, w_in_ref, w_out_ref, o_ref, k):
    x = x_ref[...]
    shift = mod_ref[3 * k:3 * k + 1, :]
    scale = mod_ref[3 * k + 1:3 * k + 2, :]
    gate = mod_ref[3 * k + 2:3 * k + 3, :]
    h = _modulated_norm(x, g_ref[k:k + 1, :], shift, scale).astype(_BF16)
    acc = jnp.zeros(x.shape, _F32)
    for c in range(N_FF_CHUNKS):
        lo = FF_CHUNK * c
        gt = _dot(h, w_in_ref[:, lo:lo + FF_CHUNK])
        up = _dot(h, w_in_ref[:, D_FF + lo:D_FF + lo + FF_CHUNK])
        a = (_silu(gt) * up).astype(_BF16)
        acc = acc + _dot(a, w_out_ref[lo:lo + FF_CHUNK, :])
    o_ref[...] = x + (0.5 * gate) * acc


def _ffn_kernel(xc_ref, xl_ref, mod_ref, g_ref, w_in_ref, w_out_ref, oc_ref, ol_ref, *, k, n_ctx_tiles):
    is_ctx = pl.program_id(0) < n_ctx_tiles

    @pl.when(is_ctx)
    def _():
        _ffn_tile(xc_ref, mod_ref, g_ref, w_in_ref, w_out_ref, oc_ref, k)

    @pl.when(jnp.logical_not(is_ctx))
    def _():
        _ffn_tile(xl_ref, mod_ref, g_ref, w_in_ref, w_out_ref, ol_ref, k)


def _ffn(xc2d, xl2d, mods, layer, ctx_row, lat_len, k, norm_g, w_in, w_out, tm):
    nc, nl = xc2d.shape[0] // tm, xl2d.shape[0] // tm
    tiles_per_seq = lat_len // tm
    which = k // 2
    ctx_tile = lambda i: (jnp.minimum(i, nc - 1), 0)
    lat_tile = lambda i: (jnp.maximum(i - nc, 0), 0)
    mod_row = lambda i: jnp.where(i < nc, ctx_row, jnp.maximum(i - nc, 0) // tiles_per_seq)
    return pl.pallas_call(
        functools.partial(_ffn_kernel, k=k, n_ctx_tiles=nc),
        grid=(nc + nl,),
        in_specs=[
            pl.BlockSpec((tm, D_MODEL), ctx_tile),
            pl.BlockSpec((tm, D_MODEL), lat_tile),
            pl.BlockSpec((None, None, N_MOD, D_MODEL), lambda i: (layer, mod_row(i), 0, 0)),
            pl.BlockSpec((None, 3, D_MODEL), lambda i: (layer, 0, 0)),
            _resident((None, None, D_MODEL, 2 * D_FF), lambda i: (layer, which, 0, 0)),
            _resident((None, None, D_FF, D_MODEL), lambda i: (layer, which, 0, 0)),
        ],
        out_specs=[pl.BlockSpec((tm, D_MODEL), ctx_tile), pl.BlockSpec((tm, D_MODEL), lat_tile)],
        out_shape=[jax.ShapeDtypeStruct(xc2d.shape, _F32), jax.ShapeDtypeStruct(xl2d.shape, _F32)],
        name="ffn_half_step",
        compiler_params=pltpu.CompilerParams(
            dimension_semantics=("arbitrary",), vmem_limit_bytes=VMEM_LIMIT),
    )(xc2d, xl2d, mods, norm_g, w_in, w_out)


def _rotate(v, cos, sin_lo, sin_hi):
    n = v.shape[-1]
    quarter = HEAD_DIM // 4
    return v * cos + pltpu.roll(v, n - quarter, 1) * sin_lo + pltpu.roll(v, quarter, 1) * sin_hi


def _proj_rows(rs, refs, latent):
    if latent:
        (x_ref, mod_ref, g_ref, w_ref, gq_ref, gk_ref, seg_ref, cos_ref, slo_ref, shi_ref,
         q_ref, kd_ref, vd_ref, mix_ref) = refs
    else:
        (x_ref, mod_ref, g_ref, w_ref, gq_ref, gk_ref, seg_ref,
         q_ref, kd_ref, vd_ref, mix_ref, k32_ref, v32_ref) = refs
    x = x_ref[rs, :]
    h = _modulated_norm(x, g_ref[1:2, :], mod_ref[3:4, :], mod_ref[4:5, :]).astype(_BF16)
    p = _dot(h, w_ref[...])
    q = p[:, :ATTN_WIDTH]
    kk = p[:, ATTN_WIDTH:ATTN_WIDTH + KV_WIDTH]
    v = p[:, ATTN_WIDTH + KV_WIDTH:ATTN_WIDTH + 2 * KV_WIDTH]
    base = ATTN_WIDTH + 2 * KV_WIDTH
    u_pool = p[:, base:base + POOL_WIDTH]
    u_conv = p[:, base + POOL_WIDTH:base + POOL_WIDTH + CONV_WIDTH]
    bgate = p[:, base + POOL_WIDTH + CONV_WIDTH:base + POOL_WIDTH + 2 * CONV_WIDTH]
    cgate = p[:, base + POOL_WIDTH + 2 * CONV_WIDTH:]

    def head_norm(t, seg):
        return t * lax.rsqrt(_dot((t * t).astype(_BF16), seg) + EPS)

    qn = jnp.concatenate(
        [head_norm(q[:, j * MXU_COLS:(j + 1) * MXU_COLS], seg_ref[...]) for j in range(ATTN_WIDTH // MXU_COLS)],
        axis=1) * gq_ref[...]
    kn = head_norm(kk, seg_ref[:KV_WIDTH, :KV_WIDTH]) * gk_ref[...]
    if not latent:
        k32_ref[rs, :] = kn
        v32_ref[rs, :] = v
    else:
        cos, slo, shi = cos_ref[rs, :], slo_ref[rs, :], shi_ref[rs, :]
        tile4 = lambda t: jnp.concatenate([t] * (ATTN_WIDTH // KV_WIDTH), axis=1)
        qn = _rotate(qn, tile4(cos), tile4(slo), tile4(shi))
        kn = _rotate(kn, cos, slo, shi)
    q_ref[rs, :] = (qn * (HEAD_DIM ** -0.5 * LOG2E)).astype(_BF16)

    low = lax.broadcasted_iota(jnp.int32, kn.shape, 1) < HEAD_DIM
    ksw = pltpu.roll(kn, HEAD_DIM, 1)
    kd_ref[rs, :KV_WIDTH] = jnp.where(low, kn, ksw).astype(_BF16)
    kd_ref[rs, KV_WIDTH:] = jnp.where(low, ksw, kn).astype(_BF16)
    vd_ref[rs, :LANES] = jnp.where(low, v, 1.0).astype(_BF16)
    vd_ref[rs, LANES:] = jnp.where(low, pltpu.roll(v, HEAD_DIM, 1), 1.0).astype(_BF16)
    mix_ref[rs, :POOL_WIDTH] = u_pool
    mix_ref[rs, POOL_WIDTH:POOL_WIDTH + CONV_WIDTH] = cgate * u_conv
    mix_ref[rs, POOL_WIDTH + CONV_WIDTH:] = bgate


def _proj_kernel(*refs, latent, sub_rows):
    for r0 in range(0, refs[0].shape[0], sub_rows):
        _proj_rows(slice(r0, r0 + sub_rows), refs, latent)


def _proj(x2d, mods, layer, mod_row_of_tile, norm_g, w, gq, gk, seg, rope, seq_len, tm, latent):
    n_tok = x2d.shape[0]
    const = lambda i: (0, 0)
    row = lambda i: (i, 0)
    per_layer = lambda i: (layer, 0, 0)
    in_specs = [
        pl.BlockSpec((tm, D_MODEL), row),
        pl.BlockSpec((None, None, N_MOD, D_MODEL), lambda i: (layer, mod_row_of_tile(i), 0, 0)),
        pl.BlockSpec((None, 3, D_MODEL), per_layer),
        _resident((None, D_MODEL, IN_WIDTH), per_layer),
        pl.BlockSpec((None, 1, ATTN_WIDTH), per_layer),
        pl.BlockSpec((None, 1, KV_WIDTH), per_layer),
        _resident((MXU_COLS, MXU_COLS), const),
    ]
    args = [x2d, mods, norm_g, w, gq, gk, seg]
    out_specs = [
        pl.BlockSpec((tm, ATTN_WIDTH), row),
        pl.BlockSpec((tm, K_SLAB), row),
        pl.BlockSpec((tm, V_SLAB), row),
        pl.BlockSpec((tm, MIX_IN_WIDTH), row),
    ]
    out_shape = [
        jax.ShapeDtypeStruct((n_tok, ATTN_WIDTH), _BF16),
        jax.ShapeDtypeStruct((n_tok, K_SLAB), _BF16),
        jax.ShapeDtypeStruct((n_tok, V_SLAB), _BF16),
        jax.ShapeDtypeStruct((n_tok, MIX_IN_WIDTH), _F32),
    ]
    if latent:
        tiles_per_seq = seq_len // tm
        pos = lambda i: (i % tiles_per_seq, 0)
        in_specs += [pl.BlockSpec((tm, KV_WIDTH), pos)] * 3
        args += list(rope)
    else:
        out_specs += [pl.BlockSpec((tm, KV_WIDTH), row)] * 2
        out_shape += [jax.ShapeDtypeStruct((n_tok, KV_WIDTH), _F32)] * 2
    return pl.pallas_call(
        functools.partial(_proj_kernel, latent=latent, sub_rows=PROJ_SUB_ROWS),
        grid=(n_tok // tm,),
        in_specs=in_specs,
        out_specs=out_specs,
        out_shape=out_shape,
        name="mixer_proj_latent" if latent else "mixer_proj_context",
        compiler_params=pltpu.CompilerParams(
            dimension_semantics=("arbitrary",), vmem_limit_bytes=VMEM_LIMIT),
    )(*args)


def _block_row(jb):
    return jb * BLOCK if isinstance(jb, int) else pl.multiple_of(jb * BLOCK, BLOCK)


def _sink_row(io, kh):
    return jnp.concatenate(
        [jnp.full((1, BLOCK), io['sink'][kh * GQA_GROUP + g] * LOG2E, _F32) for g in range(GQA_GROUP)], axis=1)


def _score_stage(jb, kh, io, s_ref, m_ref):
    latent, seq_len, tile_start, low = io['latent'], io['seq_len'], io['tile_start'], io['low']
    row0 = _block_row(jb)
    qb = io['q'][pl.ds(row0, BLOCK), kh * GQA_GROUP * HEAD_DIM:(kh + 1) * GQA_GROUP * HEAD_DIM]
    parts = []
    for g in range(GQA_GROUP):
        pair = qb[:, (g // 2) * LANES:(g // 2 + 1) * LANES]
        keep = low if g % 2 == 0 else jnp.logical_not(low)
        parts.append(jnp.where(keep, pair, jnp.zeros_like(pair)))
    qs = jnp.concatenate(parts, axis=0)
    ksl = slice(kh * KV_WIDTH, (kh + 1) * KV_WIDTH)
    if latent:
        q0 = tile_start + row0
        kstart = pl.multiple_of(jnp.clip(q0 - BLOCK, 0, seq_len - LOCAL_LEN), BLOCK)
        variant = jnp.where(q0 == 0, 0, jnp.where(q0 == seq_len - BLOCK, 2, 1))
        bias = io['bias'][variant]
        s_loc = _dot_nt(io['kd'][pl.ds(kstart, LOCAL_LEN), ksl], qs) + jnp.concatenate([bias] * GQA_GROUP, axis=1)
        pieces = [s_loc, _dot_nt(io['ckd'][:, ksl], qs)]
    else:
        pieces = [_dot_nt(io['kd'][:, ksl], qs)]
    m = _sink_row(io, kh)
    off = 0
    for piece in pieces:
        m = jnp.maximum(m, jnp.max(piece, axis=0, keepdims=True))
        s_ref[off:off + piece.shape[0], :] = piece
        off += piece.shape[0]
    m_ref[...] = jnp.broadcast_to(m, m_ref.shape)


def _exp_stage(kh, io, s_ref, m_ref, p_ref, e_ref):
    m = m_ref[0:1, :]
    for r0 in range(0, s_ref.shape[0], BLOCK):
        p_ref[r0:r0 + BLOCK, :] = jnp.exp2(s_ref[r0:r0 + BLOCK, :] - m).astype(_BF16)
    e_ref[...] = jnp.broadcast_to(jnp.exp2(_sink_row(io, kh) - m), e_ref.shape)


def _value_stage(jb, kh, io, p_ref, e_ref):
    latent, seq_len, tile_start = io['latent'], io['seq_len'], io['tile_start']
    row0 = _block_row(jb)
    vsl = slice(kh * LANES, (kh + 1) * LANES)
    if latent:
        kstart = pl.multiple_of(jnp.clip(tile_start + row0 - BLOCK, 0, seq_len - LOCAL_LEN), BLOCK)
        acc = (_dot_tn(io['vd'][pl.ds(kstart, LOCAL_LEN), vsl], p_ref[:LOCAL_LEN, :])
               + _dot_tn(io['cvd'][:, vsl], p_ref[LOCAL_LEN:, :]))
    else:
        acc = _dot_tn(io['vd'][:, vsl], p_ref[...])
    den = acc[HEAD_DIM:HEAD_DIM + 1, :] + e_ref[0:1, :]
    o = acc[:HEAD_DIM, :] / den
    for j in range(GQA_GROUP // 2):
        pair = jnp.concatenate([o[:, 2 * j * BLOCK:(2 * j + 1) * BLOCK], o[:, (2 * j + 1) * BLOCK:(2 * j + 2) * BLOCK]],
                               axis=0)
        col = (kh * (GQA_GROUP // 2) + j) * LANES
        io['y'][pl.ds(row0, BLOCK), col:col + LANES] = pair.T.astype(_BF16)


def _pool_conv(mix_ref, prev_ref, next_ref, first_tile, last_tile, tile_start, seq_len,
               pool_w_ref, pool_scale_ref, conv_w_ref):
    tq = mix_ref.shape[0]
    prev = jnp.where(first_tile, 0.0, prev_ref[...])
    nxt = jnp.where(last_tile, 0.0, next_ref[...])
    cur = mix_ref[...]
    pad = jnp.concatenate([prev, cur, nxt], axis=0)

    up = pad[:, :POOL_WIDTH]
    w2 = up[1:] + up[:-1]
    w4 = w2[:-2] + w2[2:]
    w8 = w4[:-4] + w4[4:]
    w16 = w8[:-8] + w8[8:]
    lane = lax.broadcasted_iota(jnp.int32, (tq, POOL_WIDTH), 1)
    group = lane // POOL_GROUP
    win = jnp.where(group == 0, w2[7:7 + tq],
                    jnp.where(group == 1, w4[6:6 + tq],
                              jnp.where(group == 2, w8[4:4 + tq], w16[:tq])))
    tpos = tile_start + lax.broadcasted_iota(jnp.int32, (tq, POOL_WIDTH), 0)
    half = jnp.left_shift(1, group)
    cnt = (jnp.minimum(tpos + half, seq_len) - jnp.maximum(tpos - half, 0)).astype(_F32)
    u = cur[:, :POOL_WIDTH]
    pooled = (win / cnt - u).astype(_BF16)
    pool = _dot(pooled, pool_w_ref[...]) * pool_scale_ref[...]

    uc = pad[:, POOL_WIDTH:POOL_WIDTH + CONV_WIDTH]
    conv = (uc[HALO - 1:HALO - 1 + tq] * conv_w_ref[0:1, :] + uc[HALO:HALO + tq] * conv_w_ref[1:2, :]
            + uc[HALO + 1:HALO + 1 + tq] * conv_w_ref[2:3, :])
    conv = cur[:, POOL_WIDTH + CONV_WIDTH:] * conv
    return pool, conv


def _mix_kernel(*refs, latent, seq_len):
    if latent:
        (sink_ref, x_ref, mod_ref, q_ref, kd_ref, vd_ref, ckd_ref, cvd_ref, bias_ref, mix_ref, prev_ref, next_ref,
         pool_w_ref, pool_scale_ref, conv_w_ref, w_out_ref, o_ref, y_ref, s0_ref, s1_ref, m0_ref, m1_ref, p0_ref, p1_ref, e0_ref, e1_ref) = refs
    else:
        (sink_ref, x_ref, mod_ref, q_ref, kd_ref, vd_ref, mix_ref, prev_ref, next_ref,
         pool_w_ref, pool_scale_ref, conv_w_ref, w_out_ref, o_ref, y_ref, s0_ref, s1_ref, m0_ref, m1_ref, p0_ref, p1_ref, e0_ref, e1_ref) = refs
        ckd_ref = cvd_ref = bias_ref = None
    tq = x_ref.shape[0]
    n_blocks = tq // BLOCK
    tile = pl.program_id(1)
    tile_start = tile * tq
    io = dict(latent=latent, seq_len=seq_len, tile_start=tile_start, sink=sink_ref, q=q_ref, kd=kd_ref, vd=vd_ref,
              ckd=ckd_ref, cvd=cvd_ref, bias=bias_ref, y=y_ref,
              low=lax.broadcasted_iota(jnp.int32, (BLOCK, LANES), 1) < HEAD_DIM)

    bufs = ((s0_ref, m0_ref, p0_ref, e0_ref), (s1_ref, m1_ref, p1_ref, e1_ref))
    score = lambda jb, kh: _score_stage(jb, kh, io, bufs[kh][0], bufs[kh][1])
    expo = lambda kh: _exp_stage(kh, io, *bufs[kh])
    value = lambda jb, kh: _value_stage(jb, kh, io, bufs[kh][2], bufs[kh][3])

    score(0, 0)
    expo(0)
    score(0, 1)

    def body(jb, carry):
        score(jb + 1, 0)
        expo(1)
        value(jb, 0)
        score(jb + 1, 1)
        expo(0)
        value(jb, 1)
        return carry

    lax.fori_loop(0, n_blocks - 1, body, 0, unroll=True)
    value(n_blocks - 1, 0)
    expo(1)
    value(n_blocks - 1, 1)

    pool, conv = _pool_conv(mix_ref, prev_ref, next_ref, tile == 0, tile == pl.num_programs(1) - 1,
                            tile_start, seq_len, pool_w_ref, pool_scale_ref, conv_w_ref)
    y_ref[:, ATTN_WIDTH:ATTN_WIDTH + POOL_WIDTH] = pool.astype(_BF16)
    y_ref[:, ATTN_WIDTH + POOL_WIDTH:] = conv.astype(_BF16)
    y = _dot(y_ref[...], w_out_ref[...])
    o_ref[...] = x_ref[...] + mod_ref[5:6, :] * y


def _mix(x3d, mods, layer, mod_row_of_batch, q, kd, vd, cache, bias, sink, mix_in, pool_w, pool_scale, conv_w, w_out,
         tq, latent):
    n_batch, seq_len, _ = x3d.shape
    n_tiles = seq_len // tq
    halo_blocks = seq_len // HALO
    per_tile = tq // HALO
    tile_map = lambda b, i: (b, i, 0)
    seq_map = lambda b, i: (b, 0, 0)
    per_layer = lambda b, i: (layer, 0, 0)
    n_keys = LOCAL_LEN + cache[0].shape[2] if latent else seq_len
    in_specs = [
        pl.BlockSpec(memory_space=pltpu.SMEM),
        pl.BlockSpec((None, tq, D_MODEL), tile_map),
        pl.BlockSpec((None, None, N_MOD, D_MODEL), lambda b, i: (layer, mod_row_of_batch(b), 0, 0)),
        pl.BlockSpec((None, tq, ATTN_WIDTH), tile_map),
        pl.BlockSpec((None, seq_len, K_SLAB), seq_map),
        pl.BlockSpec((None, seq_len, V_SLAB), seq_map),
    ]
    args = [sink, x3d, mods, q, kd, vd]
    if latent:
        ckd, cvd = cache
        in_specs += [
            pl.BlockSpec((None, None) + ckd.shape[2:], lambda b, i: (layer, b, 0, 0)),
            pl.BlockSpec((None, None) + cvd.shape[2:], lambda b, i: (layer, b, 0, 0)),
            _resident(bias.shape, lambda b, i: (0, 0, 0)),
        ]
        args += [ckd, cvd, bias]
    in_specs += [
        pl.BlockSpec((None, tq, MIX_IN_WIDTH), tile_map),
        pl.BlockSpec((None, HALO, MIX_IN_WIDTH), lambda b, i: (b, jnp.maximum(i * per_tile - 1, 0), 0)),
        pl.BlockSpec((None, HALO, MIX_IN_WIDTH),
                     lambda b, i: (b, jnp.minimum((i + 1) * per_tile, halo_blocks - 1), 0)),
        _resident((None, POOL_WIDTH, POOL_WIDTH), per_layer),
        pl.BlockSpec((None, 1, POOL_WIDTH), per_layer),
        pl.BlockSpec((None, 3, CONV_WIDTH), per_layer),
        _resident((None, D_MODEL, D_MODEL), per_layer),
    ]
    args += [mix_in, mix_in, mix_in, pool_w, pool_scale, conv_w, w_out]
    return pl.pallas_call(
        functools.partial(_mix_kernel, latent=latent, seq_len=seq_len),
        grid=(n_batch, n_tiles),
        in_specs=in_specs,
        out_specs=pl.BlockSpec((None, tq, D_MODEL), tile_map),
        out_shape=jax.ShapeDtypeStruct(x3d.shape, _F32),
        scratch_shapes=[pltpu.VMEM((tq, D_MODEL), _BF16)]
        + [pltpu.VMEM((n_keys, GQA_GROUP * BLOCK), _F32)] * 2 + [pltpu.VMEM((HALO, GQA_GROUP * BLOCK), _F32)] * 2
        + [pltpu.VMEM((n_keys, GQA_GROUP * BLOCK), _BF16)] * 2 + [pltpu.VMEM((HALO, GQA_GROUP * BLOCK), _F32)] * 2,
        name="mixer_latent" if latent else "mixer_context",
        compiler_params=pltpu.CompilerParams(
            dimension_semantics=("arbitrary", "arbitrary"), vmem_limit_bytes=VMEM_LIMIT),
    )(*args)


def _rope_tables(seq_len):
    rows = seq_len // GRID_W
    row = np.repeat(np.arange(rows), GRID_W).astype(np.float32)
    col = np.tile(np.arange(GRID_W), rows).astype(np.float32)
    half = HEAD_DIM // 2
    inv = (np.float32(ROPE_BASE) ** (-np.arange(0, half, 2, dtype=np.float32) / np.float32(half))).astype(np.float32)
    zeros = np.zeros((seq_len, half // 2), np.float32)

    def tables(pos):
        ang = (pos[:, None] * inv[None, :]).astype(np.float32).astype(np.float64)
        cos, sin = np.cos(ang).astype(np.float32), np.sin(ang).astype(np.float32)
        return (np.concatenate([cos, cos], axis=1), np.concatenate([-sin, zeros], axis=1),
                np.concatenate([zeros, sin], axis=1))

    per_head = [np.concatenate([a, b], axis=1) for a, b in zip(tables(row), tables(col))]
    return [jnp.asarray(np.concatenate([t, t], axis=1)) for t in per_head]


def _window_bias():
    r = np.arange(BLOCK)[None, None, :]
    c = np.arange(LOCAL_LEN)[None, :, None]
    v = np.arange(3)[:, None, None]
    return jnp.asarray(np.where(np.abs(c - BLOCK * v - r) <= WINDOW, 0.0, NEG).astype(np.float32))


def _key_slabs(t):
    t = jnp.swapaxes(t, 0, 1)
    return jnp.concatenate([t[..., 0, :], t[..., 0, :], t[..., 1, :], t[..., 1, :]], axis=-1).astype(_BF16)


def _value_slabs(t):
    t = jnp.swapaxes(t, 0, 1)
    ones = jnp.ones(t.shape[:3] + (HEAD_DIM,), t.dtype)
    return jnp.concatenate([t[..., 0, :], ones, t[..., 1, :], ones], axis=-1).astype(_BF16)


def kernel(x_prompt, x_sample, cache_k, cache_v, c, c_ctx, w_ada, b_ada, norm_g, w_ffn_in, w_ffn_out, w_in, w_out,
           q_norm_g, k_norm_g, sink, pool_w, pool_scale, conv_w):
    n_ctx, ctx_len, _ = x_prompt.shape
    n_lat, lat_len, _ = x_sample.shape
    ctx_row = n_lat

    cond = jnp.concatenate([c, c_ctx[None, :], jnp.zeros((COND_ROWS - n_lat - 1, D_MODEL), _F32)], axis=0)
    mods = _ada_rows(cond, w_ada, b_ada).reshape(DEPTH, COND_ROWS, N_MOD, D_MODEL)

    w_ffn_in_b = w_ffn_in.astype(_BF16)
    w_ffn_out_b = w_ffn_out.astype(_BF16)
    w_in_b = w_in.astype(_BF16)
    w_out_b = w_out.astype(_BF16)
    eye = jnp.eye(len(POOL_WINDOWS), dtype=_F32)
    pool_w_b = jnp.einsum('lgcd,gh->lgchd', pool_w, eye).reshape(DEPTH, POOL_WIDTH, POOL_WIDTH).astype(_BF16)
    head_of = np.arange(MXU_COLS) // HEAD_DIM
    seg = jnp.asarray(np.where(head_of[:, None] == head_of[None, :], 1.0 / HEAD_DIM, 0.0), _BF16)
    gq = jnp.tile(q_norm_g, (1, N_HEADS)).reshape(DEPTH, 1, ATTN_WIDTH)
    gk = jnp.tile(k_norm_g, (1, N_KV_HEADS)).reshape(DEPTH, 1, KV_WIDTH)
    pool_scale3 = pool_scale.reshape(DEPTH, 1, POOL_WIDTH)
    rope = _rope_tables(lat_len)
    bias = _window_bias()
    cache = (_key_slabs(cache_k), _value_slabs(cache_v))

    tm = 1024
    tm_proj = 2048
    tq_lat = 1024
    ctx = dict(latent=False, tq=ctx_len, n_batch=n_ctx, seq_len=ctx_len,
               tile_row=lambda i: ctx_row, batch_row=lambda b: ctx_row)
    lat = dict(latent=True, tq=tq_lat, n_batch=n_lat, seq_len=lat_len,
               tile_row=lambda i: i // (lat_len // tm_proj), batch_row=lambda b: b)
    xs = {'ctx': x_prompt.reshape(n_ctx * ctx_len, D_MODEL), 'lat': x_sample.reshape(n_lat * lat_len, D_MODEL)}
    new_k, new_v = [], []
    for l in range(DEPTH):
        xs['ctx'], xs['lat'] = _ffn(xs['ctx'], xs['lat'], mods, l, ctx_row, lat_len, 0, norm_g,
                                    w_ffn_in_b, w_ffn_out_b, tm)
        for name, gr in (('ctx', ctx), ('lat', lat)):
            n_batch, seq_len = gr['n_batch'], gr['seq_len']
            x2d = xs[name]
            outs = _proj(x2d, mods, l, gr['tile_row'], norm_g, w_in_b, gq, gk, seg, rope, seq_len, tm_proj, gr['latent'])
            q, kd, vd, mix_in = outs[:4]
            if not gr['latent']:
                new_k.append(outs[4].reshape(n_batch, seq_len, N_KV_HEADS, HEAD_DIM))
                new_v.append(outs[5].reshape(n_batch, seq_len, N_KV_HEADS, HEAD_DIM))
            to3d = lambda t: t.reshape(n_batch, seq_len, t.shape[-1])
            x3d = _mix(to3d(x2d), mods, l, gr['batch_row'], to3d(q), to3d(kd), to3d(vd),
                       cache if gr['latent'] else None, bias, sink[l], to3d(mix_in), pool_w_b, pool_scale3,
                       conv_w, w_out_b, gr['tq'], gr['latent'])
            xs[name] = x3d.reshape(n_batch * seq_len, D_MODEL)
        xs['ctx'], xs['lat'] = _ffn(xs['ctx'], xs['lat'], mods, l, ctx_row, lat_len, 2, norm_g,
                                    w_ffn_in_b, w_ffn_out_b, tm)
    return (xs['ctx'].reshape(x_prompt.shape), xs['lat'].reshape(x_sample.shape),
            jnp.stack(new_k, axis=1), jnp.stack(new_v, axis=1))
```

```python
import functools
import math

import numpy as np
import jax
import jax.numpy as jnp
from jax import lax
from jax.experimental import pallas as pl
from jax.experimental.pallas import tpu as pltpu

D_MODEL = 1024
DEPTH = 2
GRID_W = 64
HEAD_DIM = 64
ATTN_WIDTH = 512
N_HEADS = 8
N_KV_HEADS = 2
GQA_GROUP = 4
KV_WIDTH = 128
POOL_WIDTH = 256
POOL_WINDOWS = (2, 4, 8, 16)
POOL_GROUP = 64
CONV_WIDTH = 256
WINDOW = 128
BLOCK = 128
D_FF = 2816
ROPE_BASE = 10000.0
EPS = 1e-6
NEG = -1e30
N_MOD = 9
IN_WIDTH = 1792
MIX_IN_WIDTH = POOL_WIDTH + 2 * CONV_WIDTH
LOG2E = math.log2(math.e)

LANES = 128
HALO = 8
MXU_COLS = 256
FF_CHUNK = MXU_COLS
N_FF_CHUNKS = D_FF // FF_CHUNK
COND_ROWS = 16
ADA_CHUNK = 2304
PROJ_SUB_ROWS = 256
K_SLAB = 2 * KV_WIDTH
V_SLAB = 2 * LANES
LOCAL_LEN = 3 * BLOCK
VMEM_LIMIT = 56 * 1024 * 1024

_BF16 = jnp.bfloat16
_F32 = jnp.float32


def _dot(a, b):
    return jnp.dot(a, b, preferred_element_type=_F32)


def _dot_nt(a, b):
    return lax.dot_general(a, b, (((1,), (1,)), ((), ())), preferred_element_type=_F32)


def _dot_tn(a, b):
    return lax.dot_general(a, b, (((0,), (0,)), ((), ())), preferred_element_type=_F32)


def _silu(x):
    return x * (1.0 / (1.0 + jnp.exp(-x)))


def _modulated_norm(x, g, shift, scale):
    y = x * lax.rsqrt(jnp.mean(x * x, axis=-1, keepdims=True) + EPS)
    return (y * g) * (1.0 + scale) + shift


def _resident(shape, index_map):
    return pl.BlockSpec(shape, index_map, pipeline_mode=pl.Buffered(1))


def _ada_kernel(cond_ref, w_ref, b_ref, o_ref):
    s = _silu(cond_ref[...]).astype(_BF16)
    o_ref[...] = _dot(s, w_ref[...].astype(_BF16)) + b_ref[...]


def _ada_rows(cond, w_ada, b_ada):
    n_out = N_MOD * D_MODEL
    return pl.pallas_call(
        _ada_kernel,
        grid=(DEPTH, n_out // ADA_CHUNK),
        in_specs=[
            pl.BlockSpec((COND_ROWS, D_MODEL), lambda l, j: (0, 0)),
            pl.BlockSpec((None, D_MODEL, ADA_CHUNK), lambda l, j: (l, 0, j)),
            pl.BlockSpec((None, 1, ADA_CHUNK), lambda l, j: (l, 0, j)),
        ],
        out_specs=pl.BlockSpec((None, COND_ROWS, ADA_CHUNK), lambda l, j: (l, 0, j)),
        out_shape=jax.ShapeDtypeStruct((DEPTH, COND_ROWS, n_out), _F32),
        name="ada_rows",
        compiler_params=pltpu.CompilerParams(
            dimension_semantics=("arbitrary", "arbitrary"), vmem_limit_bytes=VMEM_LIMIT),
    )(cond, w_ada, b_ada.reshape(DEPTH, 1, n_out))


def _ffn_kernel(x_ref, mod_ref, g_ref, w_in_ref, w_out_ref, o_ref, a_ref, *, k):
    x = x_ref[...]
    shift = mod_ref[3 * k:3 * k + 1, :]
    scale = mod_ref[3 * k + 1:3 * k + 2, :]
    gate = mod_ref[3 * k + 2:3 * k + 3, :]
    h = _modulated_norm(x, g_ref[k:k + 1, :], shift, scale).astype(_BF16)
    for c in range(N_FF_CHUNKS):
        lo = FF_CHUNK * c
        gt = _dot(h, w_in_ref[:, lo:lo + FF_CHUNK])
        up = _dot(h, w_in_ref[:, D_FF + lo:D_FF + lo + FF_CHUNK])
        a_ref[:, lo:lo + FF_CHUNK] = (_silu(gt) * up).astype(_BF16)
    o_ref[...] = x + (0.5 * gate) * _dot(a_ref[...], w_out_ref[...])


def _ffn(x2d, mods, layer, mod_row_of_tile, k, norm_g, w_in, w_out, tm):
    n_tok = x2d.shape[0]
    which = k // 2
    return pl.pallas_call(
        functools.partial(_ffn_kernel, k=k),
        grid=(n_tok // tm,),
        in_specs=[
            pl.BlockSpec((tm, D_MODEL), lambda i: (i, 0)),
            pl.BlockSpec((None, None, N_MOD, D_MODEL), lambda i: (layer, mod_row_of_tile(i), 0, 0)),
            pl.BlockSpec((None, 3, D_MODEL), lambda i: (layer, 0, 0)),
            _resident((None, None, D_MODEL, 2 * D_FF), lambda i: (layer, which, 0, 0)),
            _resident((None, None, D_FF, D_MODEL), lambda i: (layer, which, 0, 0)),
        ],
        out_specs=pl.BlockSpec((tm, D_MODEL), lambda i: (i, 0)),
        out_shape=jax.ShapeDtypeStruct(x2d.shape, _F32),
        scratch_shapes=[pltpu.VMEM((tm, D_FF), _BF16)],
        name="ffn_half_step",
        compiler_params=pltpu.CompilerParams(
            dimension_semantics=("arbitrary",), vmem_limit_bytes=VMEM_LIMIT),
    )(x2d, mods, norm_g, w_in, w_out)


def _rotate(v, cos, sin_lo, sin_hi):
    n = v.shape[-1]
    quarter = HEAD_DIM // 4
    return v * cos + pltpu.roll(v, n - quarter, 1) * sin_lo + pltpu.roll(v, quarter, 1) * sin_hi


def _proj_rows(rs, refs, latent):
    if latent:
        (x_ref, mod_ref, g_ref, w_ref, gq_ref, gk_ref, seg_ref, cos_ref, slo_ref, shi_ref,
         q_ref, kd_ref, vd_ref, mix_ref) = refs
    else:
        (x_ref, mod_ref, g_ref, w_ref, gq_ref, gk_ref, seg_ref,
         q_ref, kd_ref, vd_ref, mix_ref, k32_ref, v32_ref) = refs
    x = x_ref[rs, :]
    h = _modulated_norm(x, g_ref[1:2, :], mod_ref[3:4, :], mod_ref[4:5, :]).astype(_BF16)
    p = _dot(h, w_ref[...])
    q = p[:, :ATTN_WIDTH]
    kk = p[:, ATTN_WIDTH:ATTN_WIDTH + KV_WIDTH]
    v = p[:, ATTN_WIDTH + KV_WIDTH:ATTN_WIDTH + 2 * KV_WIDTH]
    base = ATTN_WIDTH + 2 * KV_WIDTH
    u_pool = p[:, base:base + POOL_WIDTH]
    u_conv = p[:, base + POOL_WIDTH:base + POOL_WIDTH + CONV_WIDTH]
    bgate = p[:, base + POOL_WIDTH + CONV_WIDTH:base + POOL_WIDTH + 2 * CONV_WIDTH]
    cgate = p[:, base + POOL_WIDTH + 2 * CONV_WIDTH:]

    def head_norm(t, seg):
        return t * lax.rsqrt(_dot((t * t).astype(_BF16), seg) + EPS)

    qn = jnp.concatenate(
        [head_norm(q[:, j * MXU_COLS:(j + 1) * MXU_COLS], seg_ref[...]) for j in range(ATTN_WIDTH // MXU_COLS)],
        axis=1) * gq_ref[...]
    kn = head_norm(kk, seg_ref[:KV_WIDTH, :KV_WIDTH]) * gk_ref[...]
    if not latent:
        k32_ref[rs, :] = kn
        v32_ref[rs, :] = v
    else:
        cos, slo, shi = cos_ref[rs, :], slo_ref[rs, :], shi_ref[rs, :]
        tile4 = lambda t: jnp.concatenate([t] * (ATTN_WIDTH // KV_WIDTH), axis=1)
        qn = _rotate(qn, tile4(cos), tile4(slo), tile4(shi))
        kn = _rotate(kn, cos, slo, shi)
    q_ref[rs, :] = (qn * (HEAD_DIM ** -0.5 * LOG2E)).astype(_BF16)

    low = lax.broadcasted_iota(jnp.int32, kn.shape, 1) < HEAD_DIM
    ksw = pltpu.roll(kn, HEAD_DIM, 1)
    kd_ref[rs, :KV_WIDTH] = jnp.where(low, kn, ksw).astype(_BF16)
    kd_ref[rs, KV_WIDTH:] = jnp.where(low, ksw, kn).astype(_BF16)
    vd_ref[rs, :LANES] = jnp.where(low, v, 1.0).astype(_BF16)
    vd_ref[rs, LANES:] = jnp.where(low, pltpu.roll(v, HEAD_DIM, 1), 1.0).astype(_BF16)
    mix_ref[rs, :POOL_WIDTH] = u_pool
    mix_ref[rs, POOL_WIDTH:POOL_WIDTH + CONV_WIDTH] = cgate * u_conv
    mix_ref[rs, POOL_WIDTH + CONV_WIDTH:] = bgate


def _proj_kernel(*refs, latent, sub_rows):
    for r0 in range(0, refs[0].shape[0], sub_rows):
        _proj_rows(slice(r0, r0 + sub_rows), refs, latent)


def _proj(x2d, mods, layer, mod_row_of_tile, norm_g, w, gq, gk, seg, rope, seq_len, tm, latent):
    n_tok = x2d.shape[0]
    const = lambda i: (0, 0)
    row = lambda i: (i, 0)
    per_layer = lambda i: (layer, 0, 0)
    in_specs = [
        pl.BlockSpec((tm, D_MODEL), row),
        pl.BlockSpec((None, None, N_MOD, D_MODEL), lambda i: (layer, mod_row_of_tile(i), 0, 0)),
        pl.BlockSpec((None, 3, D_MODEL), per_layer),
        _resident((None, D_MODEL, IN_WIDTH), per_layer),
        pl.BlockSpec((None, 1, ATTN_WIDTH), per_layer),
        pl.BlockSpec((None, 1, KV_WIDTH), per_layer),
        _resident((MXU_COLS, MXU_COLS), const),
    ]
    args = [x2d, mods, norm_g, w, gq, gk, seg]
    out_specs = [
        pl.BlockSpec((tm, ATTN_WIDTH), row),
        pl.BlockSpec((tm, K_SLAB), row),
        pl.BlockSpec((tm, V_SLAB), row),
        pl.BlockSpec((tm, MIX_IN_WIDTH), row),
    ]
    out_shape = [
        jax.ShapeDtypeStruct((n_tok, ATTN_WIDTH), _BF16),
        jax.ShapeDtypeStruct((n_tok, K_SLAB), _BF16),
        jax.ShapeDtypeStruct((n_tok, V_SLAB), _BF16),
        jax.ShapeDtypeStruct((n_tok, MIX_IN_WIDTH), _F32),
    ]
    if latent:
        tiles_per_seq = seq_len // tm
        pos = lambda i: (i % tiles_per_seq, 0)
        in_specs += [pl.BlockSpec((tm, KV_WIDTH), pos)] * 3
        args += list(rope)
    else:
        out_specs += [pl.BlockSpec((tm, KV_WIDTH), row)] * 2
        out_shape += [jax.ShapeDtypeStruct((n_tok, KV_WIDTH), _F32)] * 2
    return pl.pallas_call(
        functools.partial(_proj_kernel, latent=latent, sub_rows=PROJ_SUB_ROWS),
        grid=(n_tok // tm,),
        in_specs=in_specs,
        out_specs=out_specs,
        out_shape=out_shape,
        name="mixer_proj_latent" if latent else "mixer_proj_context",
        compiler_params=pltpu.CompilerParams(
            dimension_semantics=("arbitrary",), vmem_limit_bytes=VMEM_LIMIT),
    )(*args)


def _block_row(jb):
    return jb * BLOCK if isinstance(jb, int) else pl.multiple_of(jb * BLOCK, BLOCK)


def _sink_row(io, kh):
    return jnp.concatenate(
        [jnp.full((1, BLOCK), io['sink'][kh * GQA_GROUP + g] * LOG2E, _F32) for g in range(GQA_GROUP)], axis=1)


def _score_stage(jb, kh, io, s_ref, m_ref):
    latent, seq_len, tile_start, low = io['latent'], io['seq_len'], io['tile_start'], io['low']
    row0 = _block_row(jb)
    qb = io['q'][pl.ds(row0, BLOCK), kh * GQA_GROUP * HEAD_DIM:(kh + 1) * GQA_GROUP * HEAD_DIM]
    parts = []
    for g in range(GQA_GROUP):
        pair = qb[:, (g // 2) * LANES:(g // 2 + 1) * LANES]
        keep = low if g % 2 == 0 else jnp.logical_not(low)
        parts.append(jnp.where(keep, pair, jnp.zeros_like(pair)))
    qs = jnp.concatenate(parts, axis=0)
    ksl = slice(kh * KV_WIDTH, (kh + 1) * KV_WIDTH)
    if latent:
        q0 = tile_start + row0
        kstart = pl.multiple_of(jnp.clip(q0 - BLOCK, 0, seq_len - LOCAL_LEN), BLOCK)
        variant = jnp.where(q0 == 0, 0, jnp.where(q0 == seq_len - BLOCK, 2, 1))
        bias = io['bias'][variant]
        s_loc = _dot_nt(io['kd'][pl.ds(kstart, LOCAL_LEN), ksl], qs) + jnp.concatenate([bias] * GQA_GROUP, axis=1)
        pieces = [s_loc, _dot_nt(io['ckd'][:, ksl], qs)]
    else:
        pieces = [_dot_nt(io['kd'][:, ksl], qs)]
    m = _sink_row(io, kh)
    off = 0
    for piece in pieces:
        m = jnp.maximum(m, jnp.max(piece, axis=0, keepdims=True))
        s_ref[off:off + piece.shape[0], :] = piece
        off += piece.shape[0]
    m_ref[...] = jnp.broadcast_to(m, m_ref.shape)


def _exp_stage(kh, io, s_ref, m_ref, p_ref, e_ref):
    m = m_ref[0:1, :]
    for r0 in range(0, s_ref.shape[0], BLOCK):
        p_ref[r0:r0 + BLOCK, :] = jnp.exp2(s_ref[r0:r0 + BLOCK, :] - m).astype(_BF16)
    e_ref[...] = jnp.broadcast_to(jnp.exp2(_sink_row(io, kh) - m), e_ref.shape)


def _value_stage(jb, kh, io, p_ref, e_ref):
    latent, seq_len, tile_start = io['latent'], io['seq_len'], io['tile_start']
    row0 = _block_row(jb)
    vsl = slice(kh * LANES, (kh + 1) * LANES)
    if latent:
        kstart = pl.multiple_of(jnp.clip(tile_start + row0 - BLOCK, 0, seq_len - LOCAL_LEN), BLOCK)
        acc = (_dot_tn(io['vd'][pl.ds(kstart, LOCAL_LEN), vsl], p_ref[:LOCAL_LEN, :])
               + _dot_tn(io['cvd'][:, vsl], p_ref[LOCAL_LEN:, :]))
    else:
        acc = _dot_tn(io['vd'][:, vsl], p_ref[...])
    den = acc[HEAD_DIM:HEAD_DIM + 1, :] + e_ref[0:1, :]
    o = acc[:HEAD_DIM, :] / den
    for j in range(GQA_GROUP // 2):
        pair = jnp.concatenate([o[:, 2 * j * BLOCK:(2 * j + 1) * BLOCK], o[:, (2 * j + 1) * BLOCK:(2 * j + 2) * BLOCK]],
                               axis=0)
        col = (kh * (GQA_GROUP // 2) + j) * LANES
        io['y'][pl.ds(row0, BLOCK), col:col + LANES] = pair.T.astype(_BF16)


def _pool_conv(mix_ref, prev_ref, next_ref, first_tile, last_tile, tile_start, seq_len,
               pool_w_ref, pool_scale_ref, conv_w_ref):
    tq = mix_ref.shape[0]
    prev = jnp.where(first_tile, 0.0, prev_ref[...])
    nxt = jnp.where(last_tile, 0.0, next_ref[...])
    cur = mix_ref[...]
    pad = jnp.concatenate([prev, cur, nxt], axis=0)

    up = pad[:, :POOL_WIDTH]
    a2 = up[:-1] + up[1:]
    a4 = a2[:-2] + a2[2:]
    a8 = a4[:-4, LANES:] + a4[4:, LANES:]
    first_group = lax.broadcasted_iota(jnp.int32, (tq, LANES), 1) < POOL_GROUP
    win = jnp.concatenate(
        [jnp.where(first_group, a2[HALO - 1:HALO - 1 + tq, :LANES], a4[HALO - 2:HALO - 2 + tq, :LANES]),
         jnp.where(first_group, a8[HALO - 4:HALO - 4 + tq], a8[:tq] + a8[HALO:HALO + tq])], axis=1)
    lane = lax.broadcasted_iota(jnp.int32, (tq, POOL_WIDTH), 1)
    group = lane // POOL_GROUP
    tpos = tile_start + lax.broadcasted_iota(jnp.int32, (tq, POOL_WIDTH), 0)
    half = jnp.left_shift(1, group)
    cnt = (jnp.minimum(tpos + half, seq_len) - jnp.maximum(tpos - half, 0)).astype(_F32)
    u = cur[:, :POOL_WIDTH]
    pooled = (win / cnt - u).astype(_BF16)
    pool = _dot(pooled, pool_w_ref[...]) * pool_scale_ref[...]

    uc = pad[:, POOL_WIDTH:POOL_WIDTH + CONV_WIDTH]
    conv = (uc[HALO - 1:HALO - 1 + tq] * conv_w_ref[0:1, :] + uc[HALO:HALO + tq] * conv_w_ref[1:2, :]
            + uc[HALO + 1:HALO + 1 + tq] * conv_w_ref[2:3, :])
    conv = cur[:, POOL_WIDTH + CONV_WIDTH:] * conv
    return pool, conv


def _mix_kernel(*refs, latent, seq_len):
    if latent:
        (sink_ref, x_ref, mod_ref, q_ref, kd_ref, vd_ref, ckd_ref, cvd_ref, bias_ref, mix_ref, prev_ref, next_ref,
         pool_w_ref, pool_scale_ref, conv_w_ref, w_out_ref, o_ref, y_ref, s0_ref, s1_ref, m0_ref, m1_ref, p0_ref, p1_ref, e0_ref, e1_ref) = refs
    else:
        (sink_ref, x_ref, mod_ref, q_ref, kd_ref, vd_ref, mix_ref, prev_ref, next_ref,
         pool_w_ref, pool_scale_ref, conv_w_ref, w_out_ref, o_ref, y_ref, s0_ref, s1_ref, m0_ref, m1_ref, p0_ref, p1_ref, e0_ref, e1_ref) = refs
        ckd_ref = cvd_ref = bias_ref = None
    tq = x_ref.shape[0]
    n_blocks = tq // BLOCK
    tile = pl.program_id(1)
    tile_start = tile * tq
    io = dict(latent=latent, seq_len=seq_len, tile_start=tile_start, sink=sink_ref, q=q_ref, kd=kd_ref, vd=vd_ref,
              ckd=ckd_ref, cvd=cvd_ref, bias=bias_ref, y=y_ref,
              low=lax.broadcasted_iota(jnp.int32, (BLOCK, LANES), 1) < HEAD_DIM)

    bufs = ((s0_ref, m0_ref, p0_ref, e0_ref), (s1_ref, m1_ref, p1_ref, e1_ref))
    score = lambda jb, kh: _score_stage(jb, kh, io, bufs[kh][0], bufs[kh][1])
    expo = lambda kh: _exp_stage(kh, io, *bufs[kh])
    value = lambda jb, kh: _value_stage(jb, kh, io, bufs[kh][2], bufs[kh][3])

    score(0, 0)
    expo(0)
    score(0, 1)

    def body(jb, carry):
        score(jb + 1, 0)
        expo(1)
        value(jb, 0)
        score(jb + 1, 1)
        expo(0)
        value(jb, 1)
        return carry

    lax.fori_loop(0, n_blocks - 1, body, 0, unroll=True)
    value(n_blocks - 1, 0)
    expo(1)
    value(n_blocks - 1, 1)

    pool, conv = _pool_conv(mix_ref, prev_ref, next_ref, tile == 0, tile == pl.num_programs(1) - 1,
                            tile_start, seq_len, pool_w_ref, pool_scale_ref, conv_w_ref)
    y_ref[:, ATTN_WIDTH:ATTN_WIDTH + POOL_WIDTH] = pool.astype(_BF16)
    y_ref[:, ATTN_WIDTH + POOL_WIDTH:] = conv.astype(_BF16)
    y = _dot(y_ref[...], w_out_ref[...])
    o_ref[...] = x_ref[...] + mod_ref[5:6, :] * y


def _mix(x3d, mods, layer, mod_row_of_batch, q, kd, vd, cache, bias, sink, mix_in, pool_w, pool_scale, conv_w, w_out,
         tq, latent):
    n_batch, seq_len, _ = x3d.shape
    n_tiles = seq_len // tq
    halo_blocks = seq_len // HALO
    per_tile = tq // HALO
    tile_map = lambda b, i: (b, i, 0)
    seq_map = lambda b, i: (b, 0, 0)
    per_layer = lambda b, i: (layer, 0, 0)
    n_keys = LOCAL_LEN + cache[0].shape[2] if latent else seq_len
    in_specs = [
        pl.BlockSpec(memory_space=pltpu.SMEM),
        pl.BlockSpec((None, tq, D_MODEL), tile_map),
        pl.BlockSpec((None, None, N_MOD, D_MODEL), lambda b, i: (layer, mod_row_of_batch(b), 0, 0)),
        pl.BlockSpec((None, tq, ATTN_WIDTH), tile_map),
        pl.BlockSpec((None, seq_len, K_SLAB), seq_map),
        pl.BlockSpec((None, seq_len, V_SLAB), seq_map),
    ]
    args = [sink, x3d, mods, q, kd, vd]
    if latent:
        ckd, cvd = cache
        in_specs += [
            pl.BlockSpec((None, None) + ckd.shape[2:], lambda b, i: (layer, b, 0, 0)),
            pl.BlockSpec((None, None) + cvd.shape[2:], lambda b, i: (layer, b, 0, 0)),
            _resident(bias.shape, lambda b, i: (0, 0, 0)),
        ]
        args += [ckd, cvd, bias]
    in_specs += [
        pl.BlockSpec((None, tq, MIX_IN_WIDTH), tile_map),
        pl.BlockSpec((None, HALO, MIX_IN_WIDTH), lambda b, i: (b, jnp.maximum(i * per_tile - 1, 0), 0)),
        pl.BlockSpec((None, HALO, MIX_IN_WIDTH),
                     lambda b, i: (b, jnp.minimum((i + 1) * per_tile, halo_blocks - 1), 0)),
        _resident((None, POOL_WIDTH, POOL_WIDTH), per_layer),
        pl.BlockSpec((None, 1, POOL_WIDTH), per_layer),
        pl.BlockSpec((None, 3, CONV_WIDTH), per_layer),
        _resident((None, D_MODEL, D_MODEL), per_layer),
    ]
    args += [mix_in, mix_in, mix_in, pool_w, pool_scale, conv_w, w_out]
    return pl.pallas_call(
        functools.partial(_mix_kernel, latent=latent, seq_len=seq_len),
        grid=(n_batch, n_tiles),
        in_specs=in_specs,
        out_specs=pl.BlockSpec((None, tq, D_MODEL), tile_map),
        out_shape=jax.ShapeDtypeStruct(x3d.shape, _F32),
        scratch_shapes=[pltpu.VMEM((tq, D_MODEL), _BF16)]
        + [pltpu.VMEM((n_keys, GQA_GROUP * BLOCK), _F32)] * 2 + [pltpu.VMEM((HALO, GQA_GROUP * BLOCK), _F32)] * 2
        + [pltpu.VMEM((n_keys, GQA_GROUP * BLOCK), _BF16)] * 2 + [pltpu.VMEM((HALO, GQA_GROUP * BLOCK), _F32)] * 2,
        name="mixer_latent" if latent else "mixer_context",
        compiler_params=pltpu.CompilerParams(
            dimension_semantics=("arbitrary", "arbitrary"), vmem_limit_bytes=VMEM_LIMIT),
    )(*args)


def _rope_tables(seq_len):
    rows = seq_len // GRID_W
    row = np.repeat(np.arange(rows), GRID_W).astype(np.float32)
    col = np.tile(np.arange(GRID_W), rows).astype(np.float32)
    half = HEAD_DIM // 2
    inv = (np.float32(ROPE_BASE) ** (-np.arange(0, half, 2, dtype=np.float32) / np.float32(half))).astype(np.float32)
    zeros = np.zeros((seq_len, half // 2), np.float32)

    def tables(pos):
        ang = (pos[:, None] * inv[None, :]).astype(np.float32).astype(np.float64)
        cos, sin = np.cos(ang).astype(np.float32), np.sin(ang).astype(np.float32)
        return (np.concatenate([cos, cos], axis=1), np.concatenate([-sin, zeros], axis=1),
                np.concatenate([zeros, sin], axis=1))

    per_head = [np.concatenate([a, b], axis=1) for a, b in zip(tables(row), tables(col))]
    return [jnp.asarray(np.concatenate([t, t], axis=1)) for t in per_head]


def _window_bias():
    r = np.arange(BLOCK)[None, None, :]
    c = np.arange(LOCAL_LEN)[None, :, None]
    v = np.arange(3)[:, None, None]
    return jnp.asarray(np.where(np.abs(c - BLOCK * v - r) <= WINDOW, 0.0, NEG).astype(np.float32))


def _key_slabs(t):
    t = jnp.swapaxes(t, 0, 1)
    return jnp.concatenate([t[..., 0, :], t[..., 0, :], t[..., 1, :], t[..., 1, :]], axis=-1).astype(_BF16)


def _value_slabs(t):
    t = jnp.swapaxes(t, 0, 1)
    ones = jnp.ones(t.shape[:3] + (HEAD_DIM,), t.dtype)
    return jnp.concatenate([t[..., 0, :], ones, t[..., 1, :], ones], axis=-1).astype(_BF16)


def kernel(x_prompt, x_sample, cache_k, cache_v, c, c_ctx, w_ada, b_ada, norm_g, w_ffn_in, w_ffn_out, w_in, w_out,
           q_norm_g, k_norm_g, sink, pool_w, pool_scale, conv_w):
    n_ctx, ctx_len, _ = x_prompt.shape
    n_lat, lat_len, _ = x_sample.shape
    ctx_row = n_lat

    cond = jnp.concatenate([c, c_ctx[None, :], jnp.zeros((COND_ROWS - n_lat - 1, D_MODEL), _F32)], axis=0)
    mods = _ada_rows(cond, w_ada, b_ada).reshape(DEPTH, COND_ROWS, N_MOD, D_MODEL)

    w_ffn_in_b = w_ffn_in.astype(_BF16)
    w_ffn_out_b = w_ffn_out.astype(_BF16)
    w_in_b = w_in.astype(_BF16)
    w_out_b = w_out.astype(_BF16)
    eye = jnp.eye(len(POOL_WINDOWS), dtype=_F32)
    pool_w_b = jnp.einsum('lgcd,gh->lgchd', pool_w, eye).reshape(DEPTH, POOL_WIDTH, POOL_WIDTH).astype(_BF16)
    head_of = np.arange(MXU_COLS) // HEAD_DIM
    seg = jnp.asarray(np.where(head_of[:, None] == head_of[None, :], 1.0 / HEAD_DIM, 0.0), _BF16)
    gq = jnp.tile(q_norm_g, (1, N_HEADS)).reshape(DEPTH, 1, ATTN_WIDTH)
    gk = jnp.tile(k_norm_g, (1, N_KV_HEADS)).reshape(DEPTH, 1, KV_WIDTH)
    pool_scale3 = pool_scale.reshape(DEPTH, 1, POOL_WIDTH)
    rope = _rope_tables(lat_len)
    bias = _window_bias()
    cache = (_key_slabs(cache_k), _value_slabs(cache_v))

    tm = 1024
    tm_proj = 2048
    tq_lat = 1024
    ctx = dict(latent=False, tq=ctx_len, n_batch=n_ctx, seq_len=ctx_len,
               tile_row=lambda rows: (lambda i: ctx_row), batch_row=lambda b: ctx_row)
    lat = dict(latent=True, tq=tq_lat, n_batch=n_lat, seq_len=lat_len,
               tile_row=lambda rows: (lambda i: i // (lat_len // rows)), batch_row=lambda b: b)
    xs = {'ctx': x_prompt.reshape(n_ctx * ctx_len, D_MODEL), 'lat': x_sample.reshape(n_lat * lat_len, D_MODEL)}
    new_k, new_v = [], []
    for l in range(DEPTH):
        for name, gr in (('ctx', ctx), ('lat', lat)):
            n_batch, seq_len = gr['n_batch'], gr['seq_len']
            x2d = _ffn(xs[name], mods, l, gr['tile_row'](tm), 0, norm_g, w_ffn_in_b, w_ffn_out_b, tm)
            tp = tm_proj if gr['latent'] else tm
            outs = _proj(x2d, mods, l, gr['tile_row'](tp), norm_g, w_in_b, gq, gk, seg, rope, seq_len, tp, gr['latent'])
            q, kd, vd, mix_in = outs[:4]
            if not gr['latent']:
                new_k.append(outs[4].reshape(n_batch, seq_len, N_KV_HEADS, HEAD_DIM))
                new_v.append(outs[5].reshape(n_batch, seq_len, N_KV_HEADS, HEAD_DIM))
            to3d = lambda t: t.reshape(n_batch, seq_len, t.shape[-1])
            x3d = _mix(to3d(x2d), mods, l, gr['batch_row'], to3d(q), to3d(kd), to3d(vd),
                       cache if gr['latent'] else None, bias, sink[l], to3d(mix_in), pool_w_b, pool_scale3,
                       conv_w, w_out_b, gr['tq'], gr['latent'])
            xs[name] = _ffn(x3d.reshape(n_batch * seq_len, D_MODEL), mods, l, gr['tile_row'](tm), 2, norm_g,
                            w_ffn_in_b, w_ffn_out_b, tm)
    return (xs['ctx'].reshape(x_prompt.shape), xs['lat'].reshape(x_sample.shape),
            jnp.stack(new_k, axis=1), jnp.stack(new_v, axis=1))
```

```python
import functools
import math

import numpy as np
import jax
import jax.numpy as jnp
from jax import lax
from jax.experimental import pallas as pl
from jax.experimental.pallas import tpu as pltpu

D_MODEL = 1024
DEPTH = 2
GRID_W = 64
HEAD_DIM = 64
ATTN_WIDTH = 512
N_HEADS = 8
N_KV_HEADS = 2
GQA_GROUP = 4
KV_WIDTH = 128
POOL_WIDTH = 256
POOL_WINDOWS = (2, 4, 8, 16)
POOL_GROUP = 64
CONV_WIDTH = 256
WINDOW = 128
BLOCK = 128
D_FF = 2816
ROPE_BASE = 10000.0
EPS = 1e-6
NEG = -1e30
N_MOD = 9
IN_WIDTH = 1792
MIX_IN_WIDTH = POOL_WIDTH + 2 * CONV_WIDTH
LOG2E = math.log2(math.e)

LANES = 128
HALO = 8
MXU_COLS = 256
FF_CHUNK = MXU_COLS
N_FF_CHUNKS = D_FF // FF_CHUNK
COND_ROWS = 16
ADA_CHUNK = 2304
PROJ_SUB_ROWS = 256
K_SLAB = 2 * KV_WIDTH
V_SLAB = 2 * LANES
LOCAL_LEN = 3 * BLOCK
VMEM_LIMIT = 56 * 1024 * 1024

_BF16 = jnp.bfloat16
_F32 = jnp.float32


def _dot(a, b):
    return jnp.dot(a, b, preferred_element_type=_F32)


def _dot_nt(a, b):
    return lax.dot_general(a, b, (((1,), (1,)), ((), ())), preferred_element_type=_F32)


def _dot_tn(a, b):
    return lax.dot_general(a, b, (((0,), (0,)), ((), ())), preferred_element_type=_F32)


def _silu(x):
    return x * (1.0 / (1.0 + jnp.exp(-x)))


def _modulated_norm(x, g, shift, scale):
    y = x * lax.rsqrt(jnp.mean(x * x, axis=-1, keepdims=True) + EPS)
    return (y * g) * (1.0 + scale) + shift


def _resident(shape, index_map):
    return pl.BlockSpec(shape, index_map, pipeline_mode=pl.Buffered(1))


def _ada_kernel(cond_ref, w_ref, b_ref, o_ref):
    s = _silu(cond_ref[...]).astype(_BF16)
    o_ref[...] = _dot(s, w_ref[...].astype(_BF16)) + b_ref[...]


def _ada_rows(cond, w_ada, b_ada):
    n_out = N_MOD * D_MODEL
    return pl.pallas_call(
        _ada_kernel,
        grid=(DEPTH, n_out // ADA_CHUNK),
        in_specs=[
            pl.BlockSpec((COND_ROWS, D_MODEL), lambda l, j: (0, 0)),
            pl.BlockSpec((None, D_MODEL, ADA_CHUNK), lambda l, j: (l, 0, j)),
            pl.BlockSpec((None, 1, ADA_CHUNK), lambda l, j: (l, 0, j)),
        ],
        out_specs=pl.BlockSpec((None, COND_ROWS, ADA_CHUNK), lambda l, j: (l, 0, j)),
        out_shape=jax.ShapeDtypeStruct((DEPTH, COND_ROWS, n_out), _F32),
        name="ada_rows",
        compiler_params=pltpu.CompilerParams(
            dimension_semantics=("arbitrary", "arbitrary"), vmem_limit_bytes=VMEM_LIMIT),
    )(cond, w_ada, b_ada.reshape(DEPTH, 1, n_out))


def _ffn_kernel(x_ref, mod_ref, g_ref, w_in_ref, w_out_ref, o_ref, a_ref, *, k):
    x = x_ref[...]
    shift = mod_ref[3 * k:3 * k + 1, :]
    scale = mod_ref[3 * k + 1:3 * k + 2, :]
    gate = mod_ref[3 * k + 2:3 * k + 3, :]
    h = _modulated_norm(x, g_ref[k:k + 1, :], shift, scale).astype(_BF16)
    for c in range(N_FF_CHUNKS):
        lo = FF_CHUNK * c
        gt = _dot(h, w_in_ref[:, lo:lo + FF_CHUNK])
        up = _dot(h, w_in_ref[:, D_FF + lo:D_FF + lo + FF_CHUNK])
        a_ref[:, lo:lo + FF_CHUNK] = (_silu(gt) * up).astype(_BF16)
    o_ref[...] = x + (0.5 * gate) * _dot(a_ref[...], w_out_ref[...])


def _ffn(x2d, mods, layer, mod_row_of_tile, k, norm_g, w_in, w_out, tm):
    n_tok = x2d.shape[0]
    which = k // 2
    return pl.pallas_call(
        functools.partial(_ffn_kernel, k=k),
        grid=(n_tok // tm,),
        in_specs=[
            pl.BlockSpec((tm, D_MODEL), lambda i: (i, 0)),
            pl.BlockSpec((None, None, N_MOD, D_MODEL), lambda i: (layer, mod_row_of_tile(i), 0, 0)),
            pl.BlockSpec((None, 3, D_MODEL), lambda i: (layer, 0, 0)),
            _resident((None, None, D_MODEL, 2 * D_FF), lambda i: (layer, which, 0, 0)),
            _resident((None, None, D_FF, D_MODEL), lambda i: (layer, which, 0, 0)),
        ],
        out_specs=pl.BlockSpec((tm, D_MODEL), lambda i: (i, 0)),
        out_shape=jax.ShapeDtypeStruct(x2d.shape, _F32),
        scratch_shapes=[pltpu.VMEM((tm, D_FF), _BF16)],
        name="ffn_half_step",
        compiler_params=pltpu.CompilerParams(
            dimension_semantics=("arbitrary",), vmem_limit_bytes=VMEM_LIMIT),
    )(x2d, mods, norm_g, w_in, w_out)


def _rotate(v, cos, sin_lo, sin_hi):
    n = v.shape[-1]
    quarter = HEAD_DIM // 4
    return v * cos + pltpu.roll(v, n - quarter, 1) * sin_lo + pltpu.roll(v, quarter, 1) * sin_hi


def _proj_rows(rs, refs, latent):
    if latent:
        (x_ref, mod_ref, g_ref, w_ref, gq_ref, gk_ref, seg_ref, cos_ref, slo_ref, shi_ref,
         q_ref, kd_ref, vd_ref, mix_ref) = refs
    else:
        (x_ref, mod_ref, g_ref, w_ref, gq_ref, gk_ref, seg_ref,
         q_ref, kd_ref, vd_ref, mix_ref, k32_ref, v32_ref) = refs
    x = x_ref[rs, :]
    h = _modulated_norm(x, g_ref[1:2, :], mod_ref[3:4, :], mod_ref[4:5, :]).astype(_BF16)
    p = _dot(h, w_ref[...])
    q = p[:, :ATTN_WIDTH]
    kk = p[:, ATTN_WIDTH:ATTN_WIDTH + KV_WIDTH]
    v = p[:, ATTN_WIDTH + KV_WIDTH:ATTN_WIDTH + 2 * KV_WIDTH]
    base = ATTN_WIDTH + 2 * KV_WIDTH
    u_pool = p[:, base:base + POOL_WIDTH]
    u_conv = p[:, base + POOL_WIDTH:base + POOL_WIDTH + CONV_WIDTH]
    bgate = p[:, base + POOL_WIDTH + CONV_WIDTH:base + POOL_WIDTH + 2 * CONV_WIDTH]
    cgate = p[:, base + POOL_WIDTH + 2 * CONV_WIDTH:]

    def head_norm(t, seg):
        return t * lax.rsqrt(_dot((t * t).astype(_BF16), seg) + EPS)

    qn = jnp.concatenate(
        [head_norm(q[:, j * MXU_COLS:(j + 1) * MXU_COLS], seg_ref[...]) for j in range(ATTN_WIDTH // MXU_COLS)],
        axis=1) * gq_ref[...]
    kn = head_norm(kk, seg_ref[:KV_WIDTH, :KV_WIDTH]) * gk_ref[...]
    if not latent:
        k32_ref[rs, :] = kn
        v32_ref[rs, :] = v
    else:
        cos, slo, shi = cos_ref[rs, :], slo_ref[rs, :], shi_ref[rs, :]
        tile4 = lambda t: jnp.concatenate([t] * (ATTN_WIDTH // KV_WIDTH), axis=1)
        qn = _rotate(qn, tile4(cos), tile4(slo), tile4(shi))
        kn = _rotate(kn, cos, slo, shi)
    q_ref[rs, :] = (qn * (HEAD_DIM ** -0.5 * LOG2E)).astype(_BF16)

    low = lax.broadcasted_iota(jnp.int32, kn.shape, 1) < HEAD_DIM
    ksw = pltpu.roll(kn, HEAD_DIM, 1)
    kd_ref[rs, :KV_WIDTH] = jnp.where(low, kn, ksw).astype(_BF16)
    kd_ref[rs, KV_WIDTH:] = jnp.where(low, ksw, kn).astype(_BF16)
    vd_ref[rs, :LANES] = jnp.where(low, v, 1.0).astype(_BF16)
    vd_ref[rs, LANES:] = jnp.where(low, pltpu.roll(v, HEAD_DIM, 1), 1.0).astype(_BF16)
    mix_ref[rs, :POOL_WIDTH] = u_pool
    mix_ref[rs, POOL_WIDTH:POOL_WIDTH + CONV_WIDTH] = cgate * u_conv
    mix_ref[rs, POOL_WIDTH + CONV_WIDTH:] = bgate


def _proj_kernel(*refs, latent, sub_rows):
    for r0 in range(0, refs[0].shape[0], sub_rows):
        _proj_rows(slice(r0, r0 + sub_rows), refs, latent)


def _proj(x2d, mods, layer, mod_row_of_tile, norm_g, w, gq, gk, seg, rope, seq_len, tm, latent):
    n_tok = x2d.shape[0]
    const = lambda i: (0, 0)
    row = lambda i: (i, 0)
    per_layer = lambda i: (layer, 0, 0)
    in_specs = [
        pl.BlockSpec((tm, D_MODEL), row),
        pl.BlockSpec((None, None, N_MOD, D_MODEL), lambda i: (layer, mod_row_of_tile(i), 0, 0)),
        pl.BlockSpec((None, 3, D_MODEL), per_layer),
        _resident((None, D_MODEL, IN_WIDTH), per_layer),
        pl.BlockSpec((None, 1, ATTN_WIDTH), per_layer),
        pl.BlockSpec((None, 1, KV_WIDTH), per_layer),
        _resident((MXU_COLS, MXU_COLS), const),
    ]
    args = [x2d, mods, norm_g, w, gq, gk, seg]
    out_specs = [
        pl.BlockSpec((tm, ATTN_WIDTH), row),
        pl.BlockSpec((tm, K_SLAB), row),
        pl.BlockSpec((tm, V_SLAB), row),
        pl.BlockSpec((tm, MIX_IN_WIDTH), row),
    ]
    out_shape = [
        jax.ShapeDtypeStruct((n_tok, ATTN_WIDTH), _BF16),
        jax.ShapeDtypeStruct((n_tok, K_SLAB), _BF16),
        jax.ShapeDtypeStruct((n_tok, V_SLAB), _BF16),
        jax.ShapeDtypeStruct((n_tok, MIX_IN_WIDTH), _F32),
    ]
    if latent:
        tiles_per_seq = seq_len // tm
        pos = lambda i: (i % tiles_per_seq, 0)
        in_specs += [pl.BlockSpec((tm, KV_WIDTH), pos)] * 3
        args += list(rope)
    else:
        out_specs += [pl.BlockSpec((tm, KV_WIDTH), row)] * 2
        out_shape += [jax.ShapeDtypeStruct((n_tok, KV_WIDTH), _F32)] * 2
    return pl.pallas_call(
        functools.partial(_proj_kernel, latent=latent, sub_rows=PROJ_SUB_ROWS),
        grid=(n_tok // tm,),
        in_specs=in_specs,
        out_specs=out_specs,
        out_shape=out_shape,
        name="mixer_proj_latent" if latent else "mixer_proj_context",
        compiler_params=pltpu.CompilerParams(
            dimension_semantics=("arbitrary",), vmem_limit_bytes=VMEM_LIMIT),
    )(*args)


def _block_row(jb):
    return jb * BLOCK if isinstance(jb, int) else pl.multiple_of(jb * BLOCK, BLOCK)


def _sink_row(io, kh):
    return jnp.concatenate(
        [jnp.full((1, BLOCK), io['sink'][kh * GQA_GROUP + g] * LOG2E, _F32) for g in range(GQA_GROUP)], axis=1)


def _score_stage(jb, kh, io, s_ref, m_ref):
    latent, seq_len, tile_start, low = io['latent'], io['seq_len'], io['tile_start'], io['low']
    row0 = _block_row(jb)
    qb = io['q'][pl.ds(row0, BLOCK), kh * GQA_GROUP * HEAD_DIM:(kh + 1) * GQA_GROUP * HEAD_DIM]
    parts = []
    for g in range(GQA_GROUP):
        pair = qb[:, (g // 2) * LANES:(g // 2 + 1) * LANES]
        keep = low if g % 2 == 0 else jnp.logical_not(low)
        parts.append(jnp.where(keep, pair, jnp.zeros_like(pair)))
    qs = jnp.concatenate(parts, axis=0)
    ksl = slice(kh * KV_WIDTH, (kh + 1) * KV_WIDTH)
    if latent:
        q0 = tile_start + row0
        kstart = pl.multiple_of(jnp.clip(q0 - BLOCK, 0, seq_len - LOCAL_LEN), BLOCK)
        variant = jnp.where(q0 == 0, 0, jnp.where(q0 == seq_len - BLOCK, 2, 1))
        bias = io['bias'][variant]
        s_loc = _dot_nt(io['kd'][pl.ds(kstart, LOCAL_LEN), ksl], qs) + jnp.concatenate([bias] * GQA_GROUP, axis=1)
        pieces = [s_loc, _dot_nt(io['ckd'][:, ksl], qs)]
    else:
        pieces = [_dot_nt(io['kd'][:, ksl], qs)]
    m = _sink_row(io, kh)
    off = 0
    for piece in pieces:
        m = jnp.maximum(m, jnp.max(piece, axis=0, keepdims=True))
        s_ref[off:off + piece.shape[0], :] = piece
        off += piece.shape[0]
    m_ref[...] = jnp.broadcast_to(m, m_ref.shape)


def _exp_stage(kh, io, s_ref, m_ref, p_ref, e_ref):
    m = m_ref[0:1, :]
    for r0 in range(0, s_ref.shape[0], BLOCK):
        p_ref[r0:r0 + BLOCK, :] = jnp.exp2(s_ref[r0:r0 + BLOCK, :] - m).astype(_BF16)
    e_ref[...] = jnp.broadcast_to(jnp.exp2(_sink_row(io, kh) - m), e_ref.shape)


def _value_stage(jb, kh, io, p_ref, e_ref):
    latent, seq_len, tile_start = io['latent'], io['seq_len'], io['tile_start']
    row0 = _block_row(jb)
    vsl = slice(kh * LANES, (kh + 1) * LANES)
    if latent:
        kstart = pl.multiple_of(jnp.clip(tile_start + row0 - BLOCK, 0, seq_len - LOCAL_LEN), BLOCK)
        acc = (_dot_tn(io['vd'][pl.ds(kstart, LOCAL_LEN), vsl], p_ref[:LOCAL_LEN, :])
               + _dot_tn(io['cvd'][:, vsl], p_ref[LOCAL_LEN:, :]))
    else:
        acc = _dot_tn(io['vd'][:, vsl], p_ref[...])
    den = acc[HEAD_DIM:HEAD_DIM + 1, :] + e_ref[0:1, :]
    o = acc[:HEAD_DIM, :] / den
    for j in range(GQA_GROUP // 2):
        pair = jnp.concatenate([o[:, 2 * j * BLOCK:(2 * j + 1) * BLOCK], o[:, (2 * j + 1) * BLOCK:(2 * j + 2) * BLOCK]],
                               axis=0)
        col = (kh * (GQA_GROUP // 2) + j) * LANES
        io['y'][pl.ds(row0, BLOCK), col:col + LANES] = pair.T.astype(_BF16)


def _pool_conv(mix_ref, prev_ref, next_ref, first_tile, last_tile, inv_cnt_ref,
               pool_w_ref, pool_scale_ref, conv_w_ref):
    tq = mix_ref.shape[0]
    prev = jnp.where(first_tile, 0.0, prev_ref[...])
    nxt = jnp.where(last_tile, 0.0, next_ref[...])
    cur = mix_ref[...]
    pad = jnp.concatenate([prev, cur, nxt], axis=0)

    up = pad[:, :POOL_WIDTH]
    a2 = up[:-1] + up[1:]
    a4 = a2[:-2] + a2[2:]
    a8 = a4[:-4, LANES:] + a4[4:, LANES:]
    first_group = lax.broadcasted_iota(jnp.int32, (tq, LANES), 1) < POOL_GROUP
    win = jnp.concatenate(
        [jnp.where(first_group, a2[HALO - 1:HALO - 1 + tq, :LANES], a4[HALO - 2:HALO - 2 + tq, :LANES]),
         jnp.where(first_group, a8[HALO - 4:HALO - 4 + tq], a8[:tq] + a8[HALO:HALO + tq])], axis=1)
    u = cur[:, :POOL_WIDTH]
    pooled = (win * inv_cnt_ref[...] - u).astype(_BF16)
    pool = _dot(pooled, pool_w_ref[...]) * pool_scale_ref[...]

    uc = pad[:, POOL_WIDTH:POOL_WIDTH + CONV_WIDTH]
    conv = (uc[HALO - 1:HALO - 1 + tq] * conv_w_ref[0:1, :] + uc[HALO:HALO + tq] * conv_w_ref[1:2, :]
            + uc[HALO + 1:HALO + 1 + tq] * conv_w_ref[2:3, :])
    conv = cur[:, POOL_WIDTH + CONV_WIDTH:] * conv
    return pool, conv


def _mix_kernel(*refs, latent, seq_len):
    if latent:
        (sink_ref, x_ref, mod_ref, q_ref, kd_ref, vd_ref, ckd_ref, cvd_ref, bias_ref, mix_ref, prev_ref, next_ref,
         inv_cnt_ref, pool_w_ref, pool_scale_ref, conv_w_ref, w_out_ref, o_ref, y_ref, s0_ref, s1_ref, m0_ref, m1_ref, p0_ref, p1_ref, e0_ref, e1_ref) = refs
    else:
        (sink_ref, x_ref, mod_ref, q_ref, kd_ref, vd_ref, mix_ref, prev_ref, next_ref,
         inv_cnt_ref, pool_w_ref, pool_scale_ref, conv_w_ref, w_out_ref, o_ref, y_ref, s0_ref, s1_ref, m0_ref, m1_ref, p0_ref, p1_ref, e0_ref, e1_ref) = refs
        ckd_ref = cvd_ref = bias_ref = None
    tq = x_ref.shape[0]
    n_blocks = tq // BLOCK
    tile = pl.program_id(1)
    tile_start = tile * tq
    io = dict(latent=latent, seq_len=seq_len, tile_start=tile_start, sink=sink_ref, q=q_ref, kd=kd_ref, vd=vd_ref,
              ckd=ckd_ref, cvd=cvd_ref, bias=bias_ref, y=y_ref,
              low=lax.broadcasted_iota(jnp.int32, (BLOCK, LANES), 1) < HEAD_DIM)

    bufs = ((s0_ref, m0_ref, p0_ref, e0_ref), (s1_ref, m1_ref, p1_ref, e1_ref))
    score = lambda jb, kh: _score_stage(jb, kh, io, bufs[kh][0], bufs[kh][1])
    expo = lambda kh: _exp_stage(kh, io, *bufs[kh])
    value = lambda jb, kh: _value_stage(jb, kh, io, bufs[kh][2], bufs[kh][3])

    score(0, 0)
    expo(0)
    score(0, 1)

    def body(jb, carry):
        score(jb + 1, 0)
        expo(1)
        value(jb, 0)
        score(jb + 1, 1)
        expo(0)
        value(jb, 1)
        return carry

    lax.fori_loop(0, n_blocks - 1, body, 0, unroll=True)
    value(n_blocks - 1, 0)
    expo(1)
    value(n_blocks - 1, 1)

    pool, conv = _pool_conv(mix_ref, prev_ref, next_ref, tile == 0, tile == pl.num_programs(1) - 1,
                            inv_cnt_ref, pool_w_ref, pool_scale_ref, conv_w_ref)
    y_ref[:, ATTN_WIDTH:ATTN_WIDTH + POOL_WIDTH] = pool.astype(_BF16)
    y_ref[:, ATTN_WIDTH + POOL_WIDTH:] = conv.astype(_BF16)
    y = _dot(y_ref[...], w_out_ref[...])
    o_ref[...] = x_ref[...] + mod_ref[5:6, :] * y


def _mix(x3d, mods, layer, mod_row_of_batch, q, kd, vd, cache, bias, sink, mix_in, pool_w, pool_scale, conv_w, w_out,
         tq, latent):
    n_batch, seq_len, _ = x3d.shape
    n_tiles = seq_len // tq
    halo_blocks = seq_len // HALO
    per_tile = tq // HALO
    tile_map = lambda b, i: (b, i, 0)
    seq_map = lambda b, i: (b, 0, 0)
    per_layer = lambda b, i: (layer, 0, 0)
    n_keys = LOCAL_LEN + cache[0].shape[2] if latent else seq_len
    in_specs = [
        pl.BlockSpec(memory_space=pltpu.SMEM),
        pl.BlockSpec((None, tq, D_MODEL), tile_map),
        pl.BlockSpec((None, None, N_MOD, D_MODEL), lambda b, i: (layer, mod_row_of_batch(b), 0, 0)),
        pl.BlockSpec((None, tq, ATTN_WIDTH), tile_map),
        pl.BlockSpec((None, seq_len, K_SLAB), seq_map),
        pl.BlockSpec((None, seq_len, V_SLAB), seq_map),
    ]
    args = [sink, x3d, mods, q, kd, vd]
    if latent:
        ckd, cvd = cache
        in_specs += [
            pl.BlockSpec((None, None) + ckd.shape[2:], lambda b, i: (layer, b, 0, 0)),
            pl.BlockSpec((None, None) + cvd.shape[2:], lambda b, i: (layer, b, 0, 0)),
            _resident(bias.shape, lambda b, i: (0, 0, 0)),
        ]
        args += [ckd, cvd, bias]
    in_specs += [
        pl.BlockSpec((None, tq, MIX_IN_WIDTH), tile_map),
        pl.BlockSpec((None, HALO, MIX_IN_WIDTH), lambda b, i: (b, jnp.maximum(i * per_tile - 1, 0), 0)),
        pl.BlockSpec((None, HALO, MIX_IN_WIDTH),
                     lambda b, i: (b, jnp.minimum((i + 1) * per_tile, halo_blocks - 1), 0)),
        pl.BlockSpec((tq, POOL_WIDTH), lambda b, i: (i, 0)),
        _resident((None, POOL_WIDTH, POOL_WIDTH), per_layer),
        pl.BlockSpec((None, 1, POOL_WIDTH), per_layer),
        pl.BlockSpec((None, 3, CONV_WIDTH), per_layer),
        _resident((None, D_MODEL, D_MODEL), per_layer),
    ]
    args += [mix_in, mix_in, mix_in, _pool_inv_count(seq_len), pool_w, pool_scale, conv_w, w_out]
    return pl.pallas_call(
        functools.partial(_mix_kernel, latent=latent, seq_len=seq_len),
        grid=(n_batch, n_tiles),
        in_specs=in_specs,
        out_specs=pl.BlockSpec((None, tq, D_MODEL), tile_map),
        out_shape=jax.ShapeDtypeStruct(x3d.shape, _F32),
        scratch_shapes=[pltpu.VMEM((tq, D_MODEL), _BF16)]
        + [pltpu.VMEM((n_keys, GQA_GROUP * BLOCK), _F32)] * 2 + [pltpu.VMEM((HALO, GQA_GROUP * BLOCK), _F32)] * 2
        + [pltpu.VMEM((n_keys, GQA_GROUP * BLOCK), _BF16)] * 2 + [pltpu.VMEM((HALO, GQA_GROUP * BLOCK), _F32)] * 2,
        name="mixer_latent" if latent else "mixer_context",
        compiler_params=pltpu.CompilerParams(
            dimension_semantics=("arbitrary", "arbitrary"), vmem_limit_bytes=VMEM_LIMIT),
    )(*args)


def _rope_tables(seq_len):
    rows = seq_len // GRID_W
    row = np.repeat(np.arange(rows), GRID_W).astype(np.float32)
    col = np.tile(np.arange(GRID_W), rows).astype(np.float32)
    half = HEAD_DIM // 2
    inv = (np.float32(ROPE_BASE) ** (-np.arange(0, half, 2, dtype=np.float32) / np.float32(half))).astype(np.float32)
    zeros = np.zeros((seq_len, half // 2), np.float32)

    def tables(pos):
        ang = (pos[:, None] * inv[None, :]).astype(np.float32).astype(np.float64)
        cos, sin = np.cos(ang).astype(np.float32), np.sin(ang).astype(np.float32)
        return (np.concatenate([cos, cos], axis=1), np.concatenate([-sin, zeros], axis=1),
                np.concatenate([zeros, sin], axis=1))

    per_head = [np.concatenate([a, b], axis=1) for a, b in zip(tables(row), tables(col))]
    return [jnp.asarray(np.concatenate([t, t], axis=1)) for t in per_head]


def _pool_inv_count(seq_len):
    t = np.arange(seq_len)[:, None]
    half = np.repeat(np.array(POOL_WINDOWS) // 2, POOL_GROUP)[None, :]
    cnt = np.minimum(t + half, seq_len) - np.maximum(t - half, 0)
    return jnp.asarray((1.0 / cnt).astype(np.float32))


def _window_bias():
    r = np.arange(BLOCK)[None, None, :]
    c = np.arange(LOCAL_LEN)[None, :, None]
    v = np.arange(3)[:, None, None]
    return jnp.asarray(np.where(np.abs(c - BLOCK * v - r) <= WINDOW, 0.0, NEG).astype(np.float32))


def _key_slabs(t):
    t = jnp.swapaxes(t, 0, 1)
    return jnp.concatenate([t[..., 0, :], t[..., 0, :], t[..., 1, :], t[..., 1, :]], axis=-1).astype(_BF16)


def _value_slabs(t):
    t = jnp.swapaxes(t, 0, 1)
    ones = jnp.ones(t.shape[:3] + (HEAD_DIM,), t.dtype)
    return jnp.concatenate([t[..., 0, :], ones, t[..., 1, :], ones], axis=-1).astype(_BF16)


def kernel(x_prompt, x_sample, cache_k, cache_v, c, c_ctx, w_ada, b_ada, norm_g, w_ffn_in, w_ffn_out, w_in, w_out,
           q_norm_g, k_norm_g, sink, pool_w, pool_scale, conv_w):
    n_ctx, ctx_len, _ = x_prompt.shape
    n_lat, lat_len, _ = x_sample.shape
    ctx_row = n_lat

    cond = jnp.concatenate([c, c_ctx[None, :], jnp.zeros((COND_ROWS - n_lat - 1, D_MODEL), _F32)], axis=0)
    mods = _ada_rows(cond, w_ada, b_ada).reshape(DEPTH, COND_ROWS, N_MOD, D_MODEL)

    w_ffn_in_b = w_ffn_in.astype(_BF16)
    w_ffn_out_b = w_ffn_out.astype(_BF16)
    w_in_b = w_in.astype(_BF16)
    w_out_b = w_out.astype(_BF16)
    eye = jnp.eye(len(POOL_WINDOWS), dtype=_F32)
    pool_w_b = jnp.einsum('lgcd,gh->lgchd', pool_w, eye).reshape(DEPTH, POOL_WIDTH, POOL_WIDTH).astype(_BF16)
    head_of = np.arange(MXU_COLS) // HEAD_DIM
    seg = jnp.asarray(np.where(head_of[:, None] == head_of[None, :], 1.0 / HEAD_DIM, 0.0), _BF16)
    gq = jnp.tile(q_norm_g, (1, N_HEADS)).reshape(DEPTH, 1, ATTN_WIDTH)
    gk = jnp.tile(k_norm_g, (1, N_KV_HEADS)).reshape(DEPTH, 1, KV_WIDTH)
    pool_scale3 = pool_scale.reshape(DEPTH, 1, POOL_WIDTH)
    rope = _rope_tables(lat_len)
    bias = _window_bias()
    cache = (_key_slabs(cache_k), _value_slabs(cache_v))

    tm = 1024
    tm_proj = 2048
    tq_lat = 1024
    ctx = dict(latent=False, tq=ctx_len, n_batch=n_ctx, seq_len=ctx_len,
               tile_row=lambda rows: (lambda i: ctx_row), batch_row=lambda b: ctx_row)
    lat = dict(latent=True, tq=tq_lat, n_batch=n_lat, seq_len=lat_len,
               tile_row=lambda rows: (lambda i: i // (lat_len // rows)), batch_row=lambda b: b)
    xs = {'ctx': x_prompt.reshape(n_ctx * ctx_len, D_MODEL), 'lat': x_sample.reshape(n_lat * lat_len, D_MODEL)}
    new_k, new_v = [], []
    for l in range(DEPTH):
        for name, gr in (('ctx', ctx), ('lat', lat)):
            n_batch, seq_len = gr['n_batch'], gr['seq_len']
            x2d = _ffn(xs[name], mods, l, gr['tile_row'](tm), 0, norm_g, w_ffn_in_b, w_ffn_out_b, tm)
            tp = tm_proj if gr['latent'] else tm
            outs = _proj(x2d, mods, l, gr['tile_row'](tp), norm_g, w_in_b, gq, gk, seg, rope, seq_len, tp, gr['latent'])
            q, kd, vd, mix_in = outs[:4]
            if not gr['latent']:
                new_k.append(outs[4].reshape(n_batch, seq_len, N_KV_HEADS, HEAD_DIM))
                new_v.append(outs[5].reshape(n_batch, seq_len, N_KV_HEADS, HEAD_DIM))
            to3d = lambda t: t.reshape(n_batch, seq_len, t.shape[-1])
            x3d = _mix(to3d(x2d), mods, l, gr['batch_row'], to3d(q), to3d(kd), to3d(vd),
                       cache if gr['latent'] else None, bias, sink[l], to3d(mix_in), pool_w_b, pool_scale3,
                       conv_w, w_out_b, gr['tq'], gr['latent'])
            xs[name] = _ffn(x3d.reshape(n_batch * seq_len, D_MODEL), mods, l, gr['tile_row'](tm), 2, norm_g,
                            w_ffn_in_b, w_ffn_out_b, tm)
    return (xs['ctx'].reshape(x_prompt.shape), xs['lat'].reshape(x_sample.shape),
            jnp.stack(new_k, axis=1), jnp.stack(new_v, axis=1))
```

```python
import functools
import math

import numpy as np
import jax
import jax.numpy as jnp
from jax import lax
from jax.experimental import pallas as pl
from jax.experimental.pallas import tpu as pltpu

D_MODEL = 1024
DEPTH = 2
GRID_W = 64
HEAD_DIM = 64
ATTN_WIDTH = 512
N_HEADS = 8
N_KV_HEADS = 2
GQA_GROUP = 4
KV_WIDTH = 128
POOL_WIDTH = 256
POOL_WINDOWS = (2, 4, 8, 16)
POOL_GROUP = 64
CONV_WIDTH = 256
WINDOW = 128
BLOCK = 128
D_FF = 2816
ROPE_BASE = 10000.0
EPS = 1e-6
NEG = -1e30
N_MOD = 9
IN_WIDTH = 1792
MIX_IN_WIDTH = POOL_WIDTH + 2 * CONV_WIDTH
LOG2E = math.log2(math.e)

LANES = 128
HALO = 8
MXU_COLS = 256
FF_CHUNK = MXU_COLS
N_FF_CHUNKS = D_FF // FF_CHUNK
COND_ROWS = 16
ADA_CHUNK = 2304
PROJ_SUB_ROWS = 256
K_SLAB = 2 * KV_WIDTH
V_SLAB = 2 * LANES
LOCAL_LEN = 3 * BLOCK
VMEM_LIMIT = 56 * 1024 * 1024

_BF16 = jnp.bfloat16
_F32 = jnp.float32


def _dot(a, b):
    return jnp.dot(a, b, preferred_element_type=_F32)


def _dot_nt(a, b):
    return lax.dot_general(a, b, (((1,), (1,)), ((), ())), preferred_element_type=_F32)


def _dot_tn(a, b):
    return lax.dot_general(a, b, (((0,), (0,)), ((), ())), preferred_element_type=_F32)


def _silu(x):
    return x * (1.0 / (1.0 + jnp.exp(-x)))


def _modulated_norm(x, g, shift, scale):
    y = x * lax.rsqrt(jnp.mean(x * x, axis=-1, keepdims=True) + EPS)
    return (y * g) * (1.0 + scale) + shift


def _resident(shape, index_map):
    return pl.BlockSpec(shape, index_map, pipeline_mode=pl.Buffered(1))


def _ada_kernel(cond_ref, w_ref, b_ref, o_ref):
    s = _silu(cond_ref[...]).astype(_BF16)
    o_ref[...] = _dot(s, w_ref[...].astype(_BF16)) + b_ref[...]


def _ada_rows(cond, w_ada, b_ada):
    n_out = N_MOD * D_MODEL
    return pl.pallas_call(
        _ada_kernel,
        grid=(DEPTH, n_out // ADA_CHUNK),
        in_specs=[
            pl.BlockSpec((COND_ROWS, D_MODEL), lambda l, j: (0, 0)),
            pl.BlockSpec((None, D_MODEL, ADA_CHUNK), lambda l, j: (l, 0, j)),
            pl.BlockSpec((None, 1, ADA_CHUNK), lambda l, j: (l, 0, j)),
        ],
        out_specs=pl.BlockSpec((None, COND_ROWS, ADA_CHUNK), lambda l, j: (l, 0, j)),
        out_shape=jax.ShapeDtypeStruct((DEPTH, COND_ROWS, n_out), _F32),
        name="ada_rows",
        compiler_params=pltpu.CompilerParams(
            dimension_semantics=("arbitrary", "arbitrary"), vmem_limit_bytes=VMEM_LIMIT),
    )(cond, w_ada, b_ada.reshape(DEPTH, 1, n_out))


def _ffn_kernel(x_ref, mod_ref, g_ref, w_in_ref, w_out_ref, o_ref, a_ref, *, k):
    x = x_ref[...]
    shift = mod_ref[3 * k:3 * k + 1, :]
    scale = mod_ref[3 * k + 1:3 * k + 2, :]
    gate = mod_ref[3 * k + 2:3 * k + 3, :]
    h = _modulated_norm(x, g_ref[k:k + 1, :], shift, scale).astype(_BF16)
    for c in range(N_FF_CHUNKS):
        lo = FF_CHUNK * c
        gt = _dot(h, w_in_ref[:, lo:lo + FF_CHUNK])
        up = _dot(h, w_in_ref[:, D_FF + lo:D_FF + lo + FF_CHUNK])
        a_ref[:, lo:lo + FF_CHUNK] = (_silu(gt) * up).astype(_BF16)
    o_ref[...] = x + (0.5 * gate) * _dot(a_ref[...], w_out_ref[...])


def _ffn(x2d, mods, layer, mod_row_of_tile, k, norm_g, w_in, w_out, tm):
    n_tok = x2d.shape[0]
    which = k // 2
    return pl.pallas_call(
        functools.partial(_ffn_kernel, k=k),
        grid=(n_tok // tm,),
        in_specs=[
            pl.BlockSpec((tm, D_MODEL), lambda i: (i, 0)),
            pl.BlockSpec((None, None, N_MOD, D_MODEL), lambda i: (layer, mod_row_of_tile(i), 0, 0)),
            pl.BlockSpec((None, 3, D_MODEL), lambda i: (layer, 0, 0)),
            _resident((None, None, D_MODEL, 2 * D_FF), lambda i: (layer, which, 0, 0)),
            _resident((None, None, D_FF, D_MODEL), lambda i: (layer, which, 0, 0)),
        ],
        out_specs=pl.BlockSpec((tm, D_MODEL), lambda i: (i, 0)),
        out_shape=jax.ShapeDtypeStruct(x2d.shape, _F32),
        scratch_shapes=[pltpu.VMEM((tm, D_FF), _BF16)],
        name="ffn_half_step",
        compiler_params=pltpu.CompilerParams(
            dimension_semantics=("arbitrary",), vmem_limit_bytes=VMEM_LIMIT,
            allow_input_fusion=[False, False, False, True, True]),
    )(x2d, mods, norm_g, w_in, w_out)


def _rotate(v, cos, sin_lo, sin_hi):
    n = v.shape[-1]
    quarter = HEAD_DIM // 4
    return v * cos + pltpu.roll(v, n - quarter, 1) * sin_lo + pltpu.roll(v, quarter, 1) * sin_hi


def _proj_rows(rs, refs, latent):
    if latent:
        (x_ref, mod_ref, g_ref, w_ref, gq_ref, gk_ref, seg_ref, cos_ref, slo_ref, shi_ref,
         q_ref, kd_ref, vd_ref, mix_ref) = refs
    else:
        (x_ref, mod_ref, g_ref, w_ref, gq_ref, gk_ref, seg_ref,
         q_ref, kd_ref, vd_ref, mix_ref, k32_ref, v32_ref) = refs
    x = x_ref[rs, :]
    h = _modulated_norm(x, g_ref[1:2, :], mod_ref[3:4, :], mod_ref[4:5, :]).astype(_BF16)
    p = _dot(h, w_ref[...])
    q = p[:, :ATTN_WIDTH]
    kk = p[:, ATTN_WIDTH:ATTN_WIDTH + KV_WIDTH]
    v = p[:, ATTN_WIDTH + KV_WIDTH:ATTN_WIDTH + 2 * KV_WIDTH]
    base = ATTN_WIDTH + 2 * KV_WIDTH
    u_pool = p[:, base:base + POOL_WIDTH]
    u_conv = p[:, base + POOL_WIDTH:base + POOL_WIDTH + CONV_WIDTH]
    bgate = p[:, base + POOL_WIDTH + CONV_WIDTH:base + POOL_WIDTH + 2 * CONV_WIDTH]
    cgate = p[:, base + POOL_WIDTH + 2 * CONV_WIDTH:]

    def head_norm(t, seg):
        return t * lax.rsqrt(_dot((t * t).astype(_BF16), seg) + EPS)

    qn = jnp.concatenate(
        [head_norm(q[:, j * MXU_COLS:(j + 1) * MXU_COLS], seg_ref[...]) for j in range(ATTN_WIDTH // MXU_COLS)],
        axis=1) * gq_ref[...]
    kn = head_norm(kk, seg_ref[:KV_WIDTH, :KV_WIDTH]) * gk_ref[...]
    if not latent:
        k32_ref[rs, :] = kn
        v32_ref[rs, :] = v
    else:
        cos, slo, shi = cos_ref[rs, :], slo_ref[rs, :], shi_ref[rs, :]
        tile4 = lambda t: jnp.concatenate([t] * (ATTN_WIDTH // KV_WIDTH), axis=1)
        qn = _rotate(qn, tile4(cos), tile4(slo), tile4(shi))
        kn = _rotate(kn, cos, slo, shi)
    q_ref[rs, :] = (qn * (HEAD_DIM ** -0.5 * LOG2E)).astype(_BF16)

    low = lax.broadcasted_iota(jnp.int32, kn.shape, 1) < HEAD_DIM
    ksw = pltpu.roll(kn, HEAD_DIM, 1)
    kd_ref[rs, :KV_WIDTH] = jnp.where(low, kn, ksw).astype(_BF16)
    kd_ref[rs, KV_WIDTH:] = jnp.where(low, ksw, kn).astype(_BF16)
    vd_ref[rs, :LANES] = jnp.where(low, v, 1.0).astype(_BF16)
    vd_ref[rs, LANES:] = jnp.where(low, pltpu.roll(v, HEAD_DIM, 1), 1.0).astype(_BF16)
    mix_ref[rs, :POOL_WIDTH] = u_pool
    mix_ref[rs, POOL_WIDTH:POOL_WIDTH + CONV_WIDTH] = cgate * u_conv
    mix_ref[rs, POOL_WIDTH + CONV_WIDTH:] = bgate


def _proj_kernel(*refs, latent, sub_rows):
    for r0 in range(0, refs[0].shape[0], sub_rows):
        _proj_rows(slice(r0, r0 + sub_rows), refs, latent)


def _proj(x2d, mods, layer, mod_row_of_tile, norm_g, w, gq, gk, seg, rope, seq_len, tm, latent):
    n_tok = x2d.shape[0]
    const = lambda i: (0, 0)
    row = lambda i: (i, 0)
    per_layer = lambda i: (layer, 0, 0)
    in_specs = [
        pl.BlockSpec((tm, D_MODEL), row),
        pl.BlockSpec((None, None, N_MOD, D_MODEL), lambda i: (layer, mod_row_of_tile(i), 0, 0)),
        pl.BlockSpec((None, 3, D_MODEL), per_layer),
        _resident((None, D_MODEL, IN_WIDTH), per_layer),
        pl.BlockSpec((None, 1, ATTN_WIDTH), per_layer),
        pl.BlockSpec((None, 1, KV_WIDTH), per_layer),
        _resident((MXU_COLS, MXU_COLS), const),
    ]
    args = [x2d, mods, norm_g, w, gq, gk, seg]
    out_specs = [
        pl.BlockSpec((tm, ATTN_WIDTH), row),
        pl.BlockSpec((tm, K_SLAB), row),
        pl.BlockSpec((tm, V_SLAB), row),
        pl.BlockSpec((tm, MIX_IN_WIDTH), row),
    ]
    out_shape = [
        jax.ShapeDtypeStruct((n_tok, ATTN_WIDTH), _BF16),
        jax.ShapeDtypeStruct((n_tok, K_SLAB), _BF16),
        jax.ShapeDtypeStruct((n_tok, V_SLAB), _BF16),
        jax.ShapeDtypeStruct((n_tok, MIX_IN_WIDTH), _F32),
    ]
    if latent:
        tiles_per_seq = seq_len // tm
        pos = lambda i: (i % tiles_per_seq, 0)
        in_specs += [pl.BlockSpec((tm, KV_WIDTH), pos)] * 3
        args += list(rope)
    else:
        out_specs += [pl.BlockSpec((tm, KV_WIDTH), row)] * 2
        out_shape += [jax.ShapeDtypeStruct((n_tok, KV_WIDTH), _F32)] * 2
    return pl.pallas_call(
        functools.partial(_proj_kernel, latent=latent, sub_rows=PROJ_SUB_ROWS),
        grid=(n_tok // tm,),
        in_specs=in_specs,
        out_specs=out_specs,
        out_shape=out_shape,
        name="mixer_proj_latent" if latent else "mixer_proj_context",
        compiler_params=pltpu.CompilerParams(
            dimension_semantics=("arbitrary",), vmem_limit_bytes=VMEM_LIMIT),
    )(*args)


def _block_row(jb):
    return jb * BLOCK if isinstance(jb, int) else pl.multiple_of(jb * BLOCK, BLOCK)


def _sink_row(io, kh):
    return jnp.concatenate(
        [jnp.full((1, BLOCK), io['sink'][kh * GQA_GROUP + g] * LOG2E, _F32) for g in range(GQA_GROUP)], axis=1)


def _score_stage(jb, kh, io, s_ref, m_ref):
    latent, seq_len, tile_start, low = io['latent'], io['seq_len'], io['tile_start'], io['low']
    row0 = _block_row(jb)
    qb = io['q'][pl.ds(row0, BLOCK), kh * GQA_GROUP * HEAD_DIM:(kh + 1) * GQA_GROUP * HEAD_DIM]
    parts = []
    for g in range(GQA_GROUP):
        pair = qb[:, (g // 2) * LANES:(g // 2 + 1) * LANES]
        keep = low if g % 2 == 0 else jnp.logical_not(low)
        parts.append(jnp.where(keep, pair, jnp.zeros_like(pair)))
    qs = jnp.concatenate(parts, axis=0)
    ksl = slice(kh * KV_WIDTH, (kh + 1) * KV_WIDTH)
    if latent:
        q0 = tile_start + row0
        kstart = pl.multiple_of(jnp.clip(q0 - BLOCK, 0, seq_len - LOCAL_LEN), BLOCK)
        variant = jnp.where(q0 == 0, 0, jnp.where(q0 == seq_len - BLOCK, 2, 1))
        bias = io['bias'][variant]
        s_loc = _dot_nt(io['kd'][pl.ds(kstart, LOCAL_LEN), ksl], qs) + jnp.concatenate([bias] * GQA_GROUP, axis=1)
        pieces = [s_loc, _dot_nt(io['ckd'][:, ksl], qs)]
    else:
        pieces = [_dot_nt(io['kd'][:, ksl], qs)]
    m = _sink_row(io, kh)
    off = 0
    for piece in pieces:
        m = jnp.maximum(m, jnp.max(piece, axis=0, keepdims=True))
        s_ref[off:off + piece.shape[0], :] = piece
        off += piece.shape[0]
    m_ref[...] = jnp.broadcast_to(m, m_ref.shape)


def _exp_stage(kh, io, s_ref, m_ref, p_ref, e_ref):
    m = m_ref[0:1, :]
    for r0 in range(0, s_ref.shape[0], BLOCK):
        p_ref[r0:r0 + BLOCK, :] = jnp.exp2(s_ref[r0:r0 + BLOCK, :] - m).astype(_BF16)
    e_ref[...] = jnp.broadcast_to(jnp.exp2(_sink_row(io, kh) - m), e_ref.shape)


def _value_stage(jb, kh, io, p_ref, e_ref):
    latent, seq_len, tile_start = io['latent'], io['seq_len'], io['tile_start']
    row0 = _block_row(jb)
    vsl = slice(kh * LANES, (kh + 1) * LANES)
    if latent:
        kstart = pl.multiple_of(jnp.clip(tile_start + row0 - BLOCK, 0, seq_len - LOCAL_LEN), BLOCK)
        acc = (_dot_tn(io['vd'][pl.ds(kstart, LOCAL_LEN), vsl], p_ref[:LOCAL_LEN, :])
               + _dot_tn(io['cvd'][:, vsl], p_ref[LOCAL_LEN:, :]))
    else:
        acc = _dot_tn(io['vd'][:, vsl], p_ref[...])
    den = acc[HEAD_DIM:HEAD_DIM + 1, :] + e_ref[0:1, :]
    o = acc[:HEAD_DIM, :] / den
    for j in range(GQA_GROUP // 2):
        pair = jnp.concatenate([o[:, 2 * j * BLOCK:(2 * j + 1) * BLOCK], o[:, (2 * j + 1) * BLOCK:(2 * j + 2) * BLOCK]],
                               axis=0)
        col = (kh * (GQA_GROUP // 2) + j) * LANES
        io['y'][pl.ds(row0, BLOCK), col:col + LANES] = pair.T.astype(_BF16)


def _pool_conv(mix_ref, prev_ref, next_ref, first_tile, last_tile, inv_cnt_ref,
               pool_w_ref, pool_scale_ref, conv_w_ref):
    tq = mix_ref.shape[0]
    prev = jnp.where(first_tile, 0.0, prev_ref[...])
    nxt = jnp.where(last_tile, 0.0, next_ref[...])
    cur = mix_ref[...]
    pad = jnp.concatenate([prev, cur, nxt], axis=0)

    up = pad[:, :POOL_WIDTH]
    a2 = up[:-1] + up[1:]
    a4 = a2[:-2] + a2[2:]
    a8 = a4[:-4, LANES:] + a4[4:, LANES:]
    first_group = lax.broadcasted_iota(jnp.int32, (tq, LANES), 1) < POOL_GROUP
    win = jnp.concatenate(
        [jnp.where(first_group, a2[HALO - 1:HALO - 1 + tq, :LANES], a4[HALO - 2:HALO - 2 + tq, :LANES]),
         jnp.where(first_group, a8[HALO - 4:HALO - 4 + tq], a8[:tq] + a8[HALO:HALO + tq])], axis=1)
    u = cur[:, :POOL_WIDTH]
    pooled = (win * inv_cnt_ref[...] - u).astype(_BF16)
    pool = _dot(pooled, pool_w_ref[...]) * pool_scale_ref[...]

    uc = pad[:, POOL_WIDTH:POOL_WIDTH + CONV_WIDTH]
    conv = (uc[HALO - 1:HALO - 1 + tq] * conv_w_ref[0:1, :] + uc[HALO:HALO + tq] * conv_w_ref[1:2, :]
            + uc[HALO + 1:HALO + 1 + tq] * conv_w_ref[2:3, :])
    conv = cur[:, POOL_WIDTH + CONV_WIDTH:] * conv
    return pool, conv


def _mix_kernel(*refs, latent, seq_len):
    if latent:
        (sink_ref, x_ref, mod_ref, q_ref, kd_ref, vd_ref, ckd_ref, cvd_ref, bias_ref, mix_ref, prev_ref, next_ref,
         inv_cnt_ref, pool_w_ref, pool_scale_ref, conv_w_ref, w_out_ref, o_ref, y_ref, s0_ref, s1_ref, m0_ref, m1_ref, p0_ref, p1_ref, e0_ref, e1_ref) = refs
    else:
        (sink_ref, x_ref, mod_ref, q_ref, kd_ref, vd_ref, mix_ref, prev_ref, next_ref,
         inv_cnt_ref, pool_w_ref, pool_scale_ref, conv_w_ref, w_out_ref, o_ref, y_ref, s0_ref, s1_ref, m0_ref, m1_ref, p0_ref, p1_ref, e0_ref, e1_ref) = refs
        ckd_ref = cvd_ref = bias_ref = None
    tq = x_ref.shape[0]
    n_blocks = tq // BLOCK
    tile = pl.program_id(1)
    tile_start = tile * tq
    io = dict(latent=latent, seq_len=seq_len, tile_start=tile_start, sink=sink_ref, q=q_ref, kd=kd_ref, vd=vd_ref,
              ckd=ckd_ref, cvd=cvd_ref, bias=bias_ref, y=y_ref,
              low=lax.broadcasted_iota(jnp.int32, (BLOCK, LANES), 1) < HEAD_DIM)

    bufs = ((s0_ref, m0_ref, p0_ref, e0_ref), (s1_ref, m1_ref, p1_ref, e1_ref))
    score = lambda jb, kh: _score_stage(jb, kh, io, bufs[kh][0], bufs[kh][1])
    expo = lambda kh: _exp_stage(kh, io, *bufs[kh])
    value = lambda jb, kh: _value_stage(jb, kh, io, bufs[kh][2], bufs[kh][3])

    score(0, 0)
    expo(0)
    score(0, 1)

    def body(jb, carry):
        score(jb + 1, 0)
        expo(1)
        value(jb, 0)
        score(jb + 1, 1)
        expo(0)
        value(jb, 1)
        return carry

    lax.fori_loop(0, n_blocks - 1, body, 0, unroll=True)
    value(n_blocks - 1, 0)
    expo(1)
    value(n_blocks - 1, 1)

    pool, conv = _pool_conv(mix_ref, prev_ref, next_ref, tile == 0, tile == pl.num_programs(1) - 1,
                            inv_cnt_ref, pool_w_ref, pool_scale_ref, conv_w_ref)
    y_ref[:, ATTN_WIDTH:ATTN_WIDTH + POOL_WIDTH] = pool.astype(_BF16)
    y_ref[:, ATTN_WIDTH + POOL_WIDTH:] = conv.astype(_BF16)
    y = _dot(y_ref[...], w_out_ref[...])
    o_ref[...] = x_ref[...] + mod_ref[5:6, :] * y


def _mix(x3d, mods, layer, mod_row_of_batch, q, kd, vd, cache, bias, sink, mix_in, pool_w, pool_scale, conv_w, w_out,
         tq, latent):
    n_batch, seq_len, _ = x3d.shape
    n_tiles = seq_len // tq
    halo_blocks = seq_len // HALO
    per_tile = tq // HALO
    tile_map = lambda b, i: (b, i, 0)
    seq_map = lambda b, i: (b, 0, 0)
    per_layer = lambda b, i: (layer, 0, 0)
    n_keys = LOCAL_LEN + cache[0].shape[2] if latent else seq_len
    in_specs = [
        pl.BlockSpec(memory_space=pltpu.SMEM),
        pl.BlockSpec((None, tq, D_MODEL), tile_map),
        pl.BlockSpec((None, None, N_MOD, D_MODEL), lambda b, i: (layer, mod_row_of_batch(b), 0, 0)),
        pl.BlockSpec((None, tq, ATTN_WIDTH), tile_map),
        pl.BlockSpec((None, seq_len, K_SLAB), seq_map),
        pl.BlockSpec((None, seq_len, V_SLAB), seq_map),
    ]
    args = [sink, x3d, mods, q, kd, vd]
    if latent:
        ckd, cvd = cache
        in_specs += [
            pl.BlockSpec((None, None) + ckd.shape[2:], lambda b, i: (layer, b, 0, 0)),
            pl.BlockSpec((None, None) + cvd.shape[2:], lambda b, i: (layer, b, 0, 0)),
            _resident(bias.shape, lambda b, i: (0, 0, 0)),
        ]
        args += [ckd, cvd, bias]
    in_specs += [
        pl.BlockSpec((None, tq, MIX_IN_WIDTH), tile_map),
        pl.BlockSpec((None, HALO, MIX_IN_WIDTH), lambda b, i: (b, jnp.maximum(i * per_tile - 1, 0), 0)),
        pl.BlockSpec((None, HALO, MIX_IN_WIDTH),
                     lambda b, i: (b, jnp.minimum((i + 1) * per_tile, halo_blocks - 1), 0)),
        pl.BlockSpec((tq, POOL_WIDTH), lambda b, i: (i, 0)),
        _resident((None, POOL_WIDTH, POOL_WIDTH), per_layer),
        pl.BlockSpec((None, 1, POOL_WIDTH), per_layer),
        pl.BlockSpec((None, 3, CONV_WIDTH), per_layer),
        _resident((None, D_MODEL, D_MODEL), per_layer),
    ]
    args += [mix_in, mix_in, mix_in, _pool_inv_count(seq_len), pool_w, pool_scale, conv_w, w_out]
    return pl.pallas_call(
        functools.partial(_mix_kernel, latent=latent, seq_len=seq_len),
        grid=(n_batch, n_tiles),
        in_specs=in_specs,
        out_specs=pl.BlockSpec((None, tq, D_MODEL), tile_map),
        out_shape=jax.ShapeDtypeStruct(x3d.shape, _F32),
        scratch_shapes=[pltpu.VMEM((tq, D_MODEL), _BF16)]
        + [pltpu.VMEM((n_keys, GQA_GROUP * BLOCK), _F32)] * 2 + [pltpu.VMEM((HALO, GQA_GROUP * BLOCK), _F32)] * 2
        + [pltpu.VMEM((n_keys, GQA_GROUP * BLOCK), _BF16)] * 2 + [pltpu.VMEM((HALO, GQA_GROUP * BLOCK), _F32)] * 2,
        name="mixer_latent" if latent else "mixer_context",
        compiler_params=pltpu.CompilerParams(
            dimension_semantics=("arbitrary", "arbitrary"), vmem_limit_bytes=VMEM_LIMIT),
    )(*args)


def _rope_tables(seq_len):
    rows = seq_len // GRID_W
    row = np.repeat(np.arange(rows), GRID_W).astype(np.float32)
    col = np.tile(np.arange(GRID_W), rows).astype(np.float32)
    half = HEAD_DIM // 2
    inv = (np.float32(ROPE_BASE) ** (-np.arange(0, half, 2, dtype=np.float32) / np.float32(half))).astype(np.float32)
    zeros = np.zeros((seq_len, half // 2), np.float32)

    def tables(pos):
        ang = (pos[:, None] * inv[None, :]).astype(np.float32).astype(np.float64)
        cos, sin = np.cos(ang).astype(np.float32), np.sin(ang).astype(np.float32)
        return (np.concatenate([cos, cos], axis=1), np.concatenate([-sin, zeros], axis=1),
                np.concatenate([zeros, sin], axis=1))

    per_head = [np.concatenate([a, b], axis=1) for a, b in zip(tables(row), tables(col))]
    return [jnp.asarray(np.concatenate([t, t], axis=1)) for t in per_head]


def _pool_inv_count(seq_len):
    t = np.arange(seq_len)[:, None]
    half = np.repeat(np.array(POOL_WINDOWS) // 2, POOL_GROUP)[None, :]
    cnt = np.minimum(t + half, seq_len) - np.maximum(t - half, 0)
    return jnp.asarray((1.0 / cnt).astype(np.float32))


def _window_bias():
    r = np.arange(BLOCK)[None, None, :]
    c = np.arange(LOCAL_LEN)[None, :, None]
    v = np.arange(3)[:, None, None]
    return jnp.asarray(np.where(np.abs(c - BLOCK * v - r) <= WINDOW, 0.0, NEG).astype(np.float32))


def _key_slabs(t):
    t = jnp.swapaxes(t, 0, 1)
    return jnp.concatenate([t[..., 0, :], t[..., 0, :], t[..., 1, :], t[..., 1, :]], axis=-1).astype(_BF16)


def _value_slabs(t):
    t = jnp.swapaxes(t, 0, 1)
    ones = jnp.ones(t.shape[:3] + (HEAD_DIM,), t.dtype)
    return jnp.concatenate([t[..., 0, :], ones, t[..., 1, :], ones], axis=-1).astype(_BF16)


def kernel(x_prompt, x_sample, cache_k, cache_v, c, c_ctx, w_ada, b_ada, norm_g, w_ffn_in, w_ffn_out, w_in, w_out,
           q_norm_g, k_norm_g, sink, pool_w, pool_scale, conv_w):
    n_ctx, ctx_len, _ = x_prompt.shape
    n_lat, lat_len, _ = x_sample.shape
    ctx_row = n_lat

    cond = jnp.concatenate([c, c_ctx[None, :], jnp.zeros((COND_ROWS - n_lat - 1, D_MODEL), _F32)], axis=0)
    mods = _ada_rows(cond, w_ada, b_ada).reshape(DEPTH, COND_ROWS, N_MOD, D_MODEL)

    w_ffn_in_b = w_ffn_in.astype(_BF16)
    w_ffn_out_b = w_ffn_out.astype(_BF16)
    w_in_b = w_in.astype(_BF16)
    w_out_b = w_out.astype(_BF16)
    eye = jnp.eye(len(POOL_WINDOWS), dtype=_F32)
    pool_w_b = jnp.einsum('lgcd,gh->lgchd', pool_w, eye).reshape(DEPTH, POOL_WIDTH, POOL_WIDTH).astype(_BF16)
    head_of = np.arange(MXU_COLS) // HEAD_DIM
    seg = jnp.asarray(np.where(head_of[:, None] == head_of[None, :], 1.0 / HEAD_DIM, 0.0), _BF16)
    gq = jnp.tile(q_norm_g, (1, N_HEADS)).reshape(DEPTH, 1, ATTN_WIDTH)
    gk = jnp.tile(k_norm_g, (1, N_KV_HEADS)).reshape(DEPTH, 1, KV_WIDTH)
    pool_scale3 = pool_scale.reshape(DEPTH, 1, POOL_WIDTH)
    rope = _rope_tables(lat_len)
    bias = _window_bias()
    cache = (_key_slabs(cache_k), _value_slabs(cache_v))

    tm = 1024
    tm_proj = 2048
    tq_lat = 1024
    ctx = dict(latent=False, tq=ctx_len, n_batch=n_ctx, seq_len=ctx_len,
               tile_row=lambda rows: (lambda i: ctx_row), batch_row=lambda b: ctx_row)
    lat = dict(latent=True, tq=tq_lat, n_batch=n_lat, seq_len=lat_len,
               tile_row=lambda rows: (lambda i: i // (lat_len // rows)), batch_row=lambda b: b)
    xs = {'ctx': x_prompt.reshape(n_ctx * ctx_len, D_MODEL), 'lat': x_sample.reshape(n_lat * lat_len, D_MODEL)}
    new_k, new_v = [], []
    for l in range(DEPTH):
        for name, gr in (('ctx', ctx), ('lat', lat)):
            n_batch, seq_len = gr['n_batch'], gr['seq_len']
            x2d = _ffn(xs[name], mods, l, gr['tile_row'](tm), 0, norm_g, w_ffn_in_b, w_ffn_out_b, tm)
            tp = tm_proj if gr['latent'] else tm
            outs = _proj(x2d, mods, l, gr['tile_row'](tp), norm_g, w_in_b, gq, gk, seg, rope, seq_len, tp, gr['latent'])
            q, kd, vd, mix_in = outs[:4]
            if not gr['latent']:
                new_k.append(outs[4].reshape(n_batch, seq_len, N_KV_HEADS, HEAD_DIM))
                new_v.append(outs[5].reshape(n_batch, seq_len, N_KV_HEADS, HEAD_DIM))
            to3d = lambda t: t.reshape(n_batch, seq_len, t.shape[-1])
            x3d = _mix(to3d(x2d), mods, l, gr['batch_row'], to3d(q), to3d(kd), to3d(vd),
                       cache if gr['latent'] else None, bias, sink[l], to3d(mix_in), pool_w_b, pool_scale3,
                       conv_w, w_out_b, gr['tq'], gr['latent'])
            xs[name] = _ffn(x3d.reshape(n_batch * seq_len, D_MODEL), mods, l, gr['tile_row'](tm), 2, norm_g,
                            w_ffn_in_b, w_ffn_out_b, tm)
    return (xs['ctx'].reshape(x_prompt.shape), xs['lat'].reshape(x_sample.shape),
            jnp.stack(new_k, axis=1), jnp.stack(new_v, axis=1))
```
